```python
import jax, jax.numpy as jnp
from jax import lax
import numpy as np

D_MODEL = 1024
BATCH = 8
SEQ = 4096
DEPTH = 1
DEC_BATCH = 8
DEC_SEQ = 16
PAST_LEN = 1024

CHUNK = 64
GMLP_CHUNK = 128
GMLP_WIDTH = D_MODEL
GMLP_GROUPS = 8
GMLP_GROUP_DIM = GMLP_WIDTH // GMLP_GROUPS
RWKV_WIDTH = D_MODEL
RWKV_HEAD_DIM = 64
RWKV_HEADS = RWKV_WIDTH // RWKV_HEAD_DIM
DECAY_LORA = 64
AAA_LORA = 64
GATE_LORA = 128
RWKV_COLS = 3 * RWKV_WIDTH + DECAY_LORA + AAA_LORA + GATE_LORA
IN_COLS = 2 * GMLP_WIDTH + RWKV_COLS + 2 * D_MODEL
IN_SPLITS = [GMLP_WIDTH, 2 * GMLP_WIDTH, 2 * GMLP_WIDTH + RWKV_COLS, 2 * GMLP_WIDTH + RWKV_COLS + D_MODEL]
RWKV_SPLITS = [RWKV_WIDTH, 2 * RWKV_WIDTH, 3 * RWKV_WIDTH, 3 * RWKV_WIDTH + DECAY_LORA, 3 * RWKV_WIDTH + DECAY_LORA + AAA_LORA]
N_MEM = 256
CA_HEADS = 4
CA_HEAD_DIM = D_MODEL // CA_HEADS
N_EXPERTS = 32
TOP_K = 4
D_EXPERT = D_MODEL
SWIGLU_LIMIT = 7.0
SWIGLU_ALPHA = 1.702
MOE_BLOCK = 128
NORM_EPS = 1e-5
GN_EPS = 64e-5

kernel_name = 'gmlp_rwkv7_memxattn_moe_stream_step'


def rms_norm(x, g):
    xf = x.astype(jnp.float32)
    y = xf * lax.rsqrt(jnp.mean(xf * xf, axis=-1, keepdims=True) + NORM_EPS)
    return (y * g.astype(jnp.float32)).astype(x.dtype)


def gmlp_spatial(v, w_s, b_s):
    B, T, G, C = v.shape
    L = min(T, GMLP_CHUNK)
    n_blocks = T // L
    mask = jnp.tril(jnp.ones((L, L), dtype=bool))
    w = jnp.where(mask[None], w_s[:, :L, :L], 0).astype(v.dtype)
    vc = v.reshape(B, n_blocks, L, G, C)
    out = jnp.einsum('gij,bnjgc->bnigc', w, vc) + b_s[:, :L].T[None, None, :, :, None].astype(v.dtype)
    return out.reshape(B, T, G, C)


def rwkv7_scan(s0, r, decay, k, v, kk, a):
    def step(S, inp):
        r_t, d_t, k_t, v_t, kk_t, a_t = inp
        sa = jnp.einsum('bhvk,bhk->bhv', S, -kk_t)
        S = S * d_t[:, :, None, :] + sa[..., None] * (kk_t * a_t)[:, :, None, :] + v_t[..., None] * k_t[:, :, None, :]
        y = jnp.einsum('bhvk,bhk->bhv', S, r_t)
        return S, y
    xs = tuple(jnp.moveaxis(z, 1, 0) for z in (r, decay, k, v, kk, a))
    S, ys = lax.scan(step, s0, xs)
    return jnp.moveaxis(ys, 0, 1), S


def rwkv7_mix(rw, s0, p):
    B, T, _ = rw.shape
    f32 = jnp.float32
    r, k, v, wl, al, gl = jnp.split(rw, RWKV_SPLITS, axis=-1)
    w = -jax.nn.softplus(-(p['rwkv_w0'] + jnp.tanh(wl) @ p['rwkv_w2']).astype(f32)) - 0.5
    a = jax.nn.sigmoid((p['rwkv_a0'] + al @ p['rwkv_a2']).astype(f32))
    g = jax.nn.sigmoid(gl) @ p['rwkv_g2']
    heads = lambda z: z.reshape(B, T, RWKV_HEADS, RWKV_HEAD_DIM)
    kk = heads((k * p['rwkv_k_k']).astype(f32))
    kk = kk / jnp.maximum(jnp.sqrt(jnp.sum(kk * kk, axis=-1, keepdims=True)), 1e-12)
    kf = k.astype(f32) * (1.0 + (a - 1.0) * p['rwkv_k_a'].astype(f32))
    decay = jnp.exp(-jnp.exp(w))
    rh, kh, vh, ah = heads(r.astype(f32)), heads(kf), heads(v.astype(f32)), heads(a)
    o, s = rwkv7_scan(s0.astype(f32), rh, heads(decay), kh, vh, kk, ah)
    mu = jnp.mean(o, axis=-1, keepdims=True)
    var = jnp.mean(jnp.square(o - mu), axis=-1, keepdims=True)
    o = (o - mu) * lax.rsqrt(var + GN_EPS)
    lnx_w = p['rwkv_lnx_w'].astype(f32).reshape(RWKV_HEADS, RWKV_HEAD_DIM)
    lnx_b = p['rwkv_lnx_b'].astype(f32).reshape(RWKV_HEADS, RWKV_HEAD_DIM)
    o = o * lnx_w + lnx_b
    o = o + jnp.sum(rh * kh * p['rwkv_r_k'].astype(f32), axis=-1, keepdims=True) * vh
    y = o.reshape(B, T, RWKV_WIDTH).astype(rw.dtype) * g
    return y, s.astype(s0.dtype)


def memory_kv(mem, g, wk, wv):
    B = mem.shape[0]
    mn = rms_norm(mem, g)
    mk = (mn @ wk).reshape(B, N_MEM, CA_HEADS, CA_HEAD_DIM)
    mv = (mn @ wv).reshape(B, N_MEM, CA_HEADS, CA_HEAD_DIM)
    return mk, mv


def memory_cross_attention(xn, mk, mv, wq, wo):
    B, T, _ = xn.shape
    q = (xn @ wq).reshape(B, T, CA_HEADS, CA_HEAD_DIM)
    s = jnp.einsum('bthd,bmhd->bhtm', q, mk.astype(q.dtype)).astype(jnp.float32) * (CA_HEAD_DIM ** -0.5)
    pr = jax.nn.softmax(s, axis=-1).astype(q.dtype)
    o = jnp.einsum('bhtm,bmhd->bthd', pr, mv.astype(q.dtype)).reshape(B, T, D_MODEL)
    return o @ wo


def moe_ffn(x2d, router_w, router_b, w_gu, b_gu, w_dn, b_dn):
    T = x2d.shape[0]
    logits = (x2d @ router_w).astype(jnp.float32) + router_b.astype(jnp.float32)
    top_v, top_e = lax.top_k(logits, TOP_K)
    gate = jax.nn.softmax(top_v, axis=-1)
    n_assign = T * TOP_K
    flat_e = top_e.reshape(-1).astype(jnp.int32)
    flat_tok = jnp.arange(n_assign, dtype=jnp.int32) // TOP_K
    flat_gate = gate.reshape(-1)
    order = jnp.argsort(flat_e)
    sorted_e = flat_e[order]
    counts = jnp.zeros((N_EXPERTS,), jnp.int32).at[flat_e].add(1)
    padded = (counts + MOE_BLOCK - 1) // MOE_BLOCK * MOE_BLOCK
    start = jnp.cumsum(counts) - counts
    pad_end = jnp.cumsum(padded)
    pad_start = pad_end - padded
    dest = pad_start[sorted_e] + jnp.arange(n_assign, dtype=jnp.int32) - start[sorted_e]
    n_rows = -(-(n_assign + N_EXPERTS * (MOE_BLOCK - 1)) // MOE_BLOCK) * MOE_BLOCK
    n_blk = n_rows // MOE_BLOCK
    row_tok = jnp.zeros((n_rows,), jnp.int32).at[dest].set(flat_tok[order])
    row_gate = jnp.zeros((n_rows,), jnp.float32).at[dest].set(flat_gate[order])
    blk_e = jnp.minimum(jnp.searchsorted(pad_end, jnp.arange(n_blk, dtype=jnp.int32) * MOE_BLOCK, side='right'), N_EXPERTS - 1)

    def expert_block(args):
        tok, e = args
        h = x2d[tok] @ w_gu[e] + b_gu[e]
        hg = jnp.minimum(h[:, :D_EXPERT], SWIGLU_LIMIT)
        hl = jnp.clip(h[:, D_EXPERT:], -SWIGLU_LIMIT, SWIGLU_LIMIT)
        act = hg * jax.nn.sigmoid(SWIGLU_ALPHA * hg) * (hl + 1.0)
        return act @ w_dn[e] + b_dn[e]

    ys = lax.map(expert_block, (row_tok.reshape(n_blk, MOE_BLOCK), blk_e)).reshape(n_rows, -1)
    ys = ys * row_gate[:, None].astype(ys.dtype)
    return jax.ops.segment_sum(ys, row_tok, num_segments=T)


def layer_forward(x, shift_prev, s0, mk, mv, p):
    B, T, _ = x.shape
    xn = rms_norm(x, p['norm_mix_g'])
    proj = xn @ p['w_in']
    u, v, rw, ga, gb = jnp.split(proj, IN_SPLITS, axis=-1)
    v_n = rms_norm(jax.nn.gelu(v, approximate=False), p['gmlp_v_norm_g'])
    sp = gmlp_spatial(v_n.reshape(B, T, GMLP_GROUPS, GMLP_GROUP_DIM), p['gmlp_ws'], p['gmlp_bs'])
    y_a = jax.nn.gelu(u, approximate=False) * sp.reshape(B, T, GMLP_WIDTH)
    rw_prev = jnp.concatenate([shift_prev.astype(rw.dtype), rw[:, :-1]], axis=1)
    rw_mix = rw + (rw_prev - rw) * p['rwkv_mu']
    y_b, s_new = rwkv7_mix(rw_mix, s0, p)
    merged = jax.nn.sigmoid(ga) * y_a + jax.nn.sigmoid(gb) * y_b
    x = x + merged @ p['w_out']
    x = x + memory_cross_attention(rms_norm(x, p['norm_ca_g']), mk, mv, p['ca_wq'], p['ca_wo'])
    h = rms_norm(x, p['norm_ffn_g']).reshape(B * T, D_MODEL)
    x = x + moe_ffn(h, p['router_w'], p['router_b'], p['moe_w_gu'], p['moe_b_gu'], p['moe_w_down'], p['moe_b_down']).reshape(B, T, D_MODEL)
    return x, rw[:, -1:], s_new, v_n


def setup_inputs(seed: int = 0) -> dict:
    key = jax.random.key(seed)
    ks = iter(jax.random.split(key, 48))
    f32 = jnp.float32

    def nrm(shape, scale):
        return scale * jax.random.normal(next(ks), shape, f32)

    D = D_MODEL
    return {
        'x_prompt': nrm((BATCH, SEQ, D), 1.0),
        'x_sample': nrm((DEC_BATCH, DEC_SEQ, D), 1.0),
        'mem_prompt': nrm((BATCH, N_MEM, D), 1.0),
        'state_shift': nrm((DEPTH, DEC_BATCH, 1, RWKV_COLS), 1.0),
        'state_rwkv': nrm((DEPTH, DEC_BATCH, RWKV_HEADS, RWKV_HEAD_DIM, RWKV_HEAD_DIM), 0.1),
        'cache_mem_k': nrm((DEPTH, DEC_BATCH, N_MEM, CA_HEADS, CA_HEAD_DIM), 1.0),
        'cache_mem_v': nrm((DEPTH, DEC_BATCH, N_MEM, CA_HEADS, CA_HEAD_DIM), 1.0),
        'norm_mix_g': 1.0 + nrm((DEPTH, D), 0.05),
        'w_in': nrm((DEPTH, D, IN_COLS), D ** -0.5),
        'gmlp_v_norm_g': 1.0 + nrm((DEPTH, GMLP_WIDTH), 0.05),
        'gmlp_ws': nrm((DEPTH, GMLP_GROUPS, GMLP_CHUNK, GMLP_CHUNK), GMLP_CHUNK ** -0.5),
        'gmlp_bs': 1.0 + nrm((DEPTH, GMLP_GROUPS, GMLP_CHUNK), 0.1),
        'rwkv_mu': 0.5 + nrm((DEPTH, RWKV_COLS), 0.15),
        'rwkv_w0': -1.0 + nrm((DEPTH, RWKV_WIDTH), 0.5),
        'rwkv_w2': nrm((DEPTH, DECAY_LORA, RWKV_WIDTH), 0.1),
        'rwkv_a0': nrm((DEPTH, RWKV_WIDTH), 0.1),
        'rwkv_a2': nrm((DEPTH, AAA_LORA, RWKV_WIDTH), 0.1),
        'rwkv_g2': nrm((DEPTH, GATE_LORA, RWKV_WIDTH), GATE_LORA ** -0.5),
        'rwkv_k_k': 0.85 + nrm((DEPTH, RWKV_WIDTH), 0.05),
        'rwkv_k_a': 1.0 + nrm((DEPTH, RWKV_WIDTH), 0.05),
        'rwkv_r_k': nrm((DEPTH, RWKV_HEADS, RWKV_HEAD_DIM), 0.1),
        'rwkv_lnx_w': 1.0 + nrm((DEPTH, RWKV_WIDTH), 0.05),
        'rwkv_lnx_b': nrm((DEPTH, RWKV_WIDTH), 0.01),
        'w_out': nrm((DEPTH, D, D), D ** -0.5),
        'norm_ca_g': 1.0 + nrm((DEPTH, D), 0.05),
        'norm_mem_g': 1.0 + nrm((DEPTH, D), 0.05),
        'ca_wq': nrm((DEPTH, D, D), D ** -0.5),
        'ca_wk': nrm((DEPTH, D, D), D ** -0.5),
        'ca_wv': nrm((DEPTH, D, D), D ** -0.5),
        'ca_wo': nrm((DEPTH, D, D), D ** -0.5),
        'norm_ffn_g': 1.0 + nrm((DEPTH, D), 0.05),
        'router_w': nrm((DEPTH, D, N_EXPERTS), D ** -0.5),
        'router_b': nrm((DEPTH, N_EXPERTS), 0.01),
        'moe_w_gu': nrm((DEPTH, N_EXPERTS, D, 2 * D_EXPERT), D ** -0.5),
        'moe_b_gu': nrm((DEPTH, N_EXPERTS, 2 * D_EXPERT), 0.01),
        'moe_w_down': nrm((DEPTH, N_EXPERTS, D_EXPERT, D), D_EXPERT ** -0.5),
        'moe_b_down': nrm((DEPTH, N_EXPERTS, D), 0.01),
        'final_norm_g': 1.0 + nrm((D,), 0.05),
    }


def reference(x_prompt, x_sample, mem_prompt, state_shift, state_rwkv, cache_mem_k, cache_mem_v,
              norm_mix_g, w_in, gmlp_v_norm_g, gmlp_ws, gmlp_bs, rwkv_mu, rwkv_w0, rwkv_w2, rwkv_a0, rwkv_a2,
              rwkv_g2, rwkv_k_k, rwkv_k_a, rwkv_r_k, rwkv_lnx_w, rwkv_lnx_b, w_out, norm_ca_g, norm_mem_g,
              ca_wq, ca_wk, ca_wv, ca_wo, norm_ffn_g, router_w, router_b, moe_w_gu, moe_b_gu, moe_w_down,
              moe_b_down, final_norm_g):
    xp, xs = x_prompt, x_sample
    Bp = xp.shape[0]
    p_shift, p_state, p_mk, p_mv = [], [], [], []
    s_shift, s_state, s_v = [], [], []
    for l in range(DEPTH):
        p = dict(norm_mix_g=norm_mix_g[l], w_in=w_in[l], gmlp_v_norm_g=gmlp_v_norm_g[l], gmlp_ws=gmlp_ws[l],
                 gmlp_bs=gmlp_bs[l], rwkv_mu=rwkv_mu[l], rwkv_w0=rwkv_w0[l], rwkv_w2=rwkv_w2[l],
                 rwkv_a0=rwkv_a0[l], rwkv_a2=rwkv_a2[l], rwkv_g2=rwkv_g2[l], rwkv_k_k=rwkv_k_k[l],
                 rwkv_k_a=rwkv_k_a[l], rwkv_r_k=rwkv_r_k[l], rwkv_lnx_w=rwkv_lnx_w[l], rwkv_lnx_b=rwkv_lnx_b[l],
                 w_out=w_out[l], norm_ca_g=norm_ca_g[l], ca_wq=ca_wq[l], ca_wo=ca_wo[l], norm_ffn_g=norm_ffn_g[l],
                 router_w=router_w[l], router_b=router_b[l], moe_w_gu=moe_w_gu[l], moe_b_gu=moe_b_gu[l],
                 moe_w_down=moe_w_down[l], moe_b_down=moe_b_down[l])
        mk, mv = memory_kv(mem_prompt, norm_mem_g[l], ca_wk[l], ca_wv[l])
        shift0 = jnp.zeros((Bp, 1, RWKV_COLS), xp.dtype)
        s_zero = jnp.zeros((Bp, RWKV_HEADS, RWKV_HEAD_DIM, RWKV_HEAD_DIM), xp.dtype)
        xp, sh_p, st_p, _ = layer_forward(xp, shift0, s_zero, mk, mv, p)
        p_shift.append(sh_p); p_state.append(st_p); p_mk.append(mk); p_mv.append(mv)
        xs, sh_s, st_s, v_s = layer_forward(xs, state_shift[l], state_rwkv[l], cache_mem_k[l], cache_mem_v[l], p)
        s_shift.append(sh_s); s_state.append(st_s); s_v.append(v_s)
    y_prompt = rms_norm(xp, final_norm_g)
    y_sample = rms_norm(xs, final_norm_g)
    return (y_prompt, y_sample, jnp.stack(p_shift), jnp.stack(p_state), jnp.stack(p_mk), jnp.stack(p_mv),
            jnp.stack(s_shift), jnp.stack(s_state), jnp.stack(s_v))
```

```python
import functools

import jax
import jax.numpy as jnp
from jax import lax
from jax.experimental import pallas as pl
from jax.experimental.pallas import tpu as pltpu

F32 = jnp.float32
BF16 = jnp.bfloat16

NORM_EPS = 1e-5
GN_EPS = 64e-5
RWKV_HEAD_DIM = 64
RWKV_CHUNK = 64
GMLP_CHUNK = 128
GMLP_GROUPS = 8
CA_HEADS = 4
N_EXPERTS = 32
TOP_K = 4
SWIGLU_LIMIT = 7.0
SWIGLU_ALPHA = 1.702
LANES = 128
MOE_BLOCK = 256
VMEM_LIMIT = 56 * 1024 * 1024


def _params(n_axes=1):
    return pltpu.CompilerParams(dimension_semantics=("arbitrary",) * n_axes, vmem_limit_bytes=VMEM_LIMIT)


def _rms(x, g):
    return x * lax.rsqrt(jnp.mean(x * x, axis=-1, keepdims=True) + NORM_EPS) * g


def _gelu(x):
    return x * (lax.erf(x * 0.7071067811865476) + 1.0) * 0.5


def _sigmoid(x):
    return 1.0 / (1.0 + jnp.exp(-x))


def _bdot(a, b):
    return jnp.dot(a.astype(BF16), b.astype(BF16), preferred_element_type=F32)


def _hi_lo(x):
    hi = x.astype(BF16)
    lo = (x - hi.astype(F32)).astype(BF16)
    return hi, lo


def _dot3(a, b, mode="nn"):
    ah, al = _hi_lo(a)
    bh, bl = _hi_lo(b)
    if mode == "nn":
        lhs = jnp.concatenate([ah, ah, al], axis=1)
        rhs = jnp.concatenate([bh, bl, bh], axis=0)
        dims = (((1,), (0,)), ((), ()))
    elif mode == "nt":
        lhs = jnp.concatenate([ah, ah, al], axis=1)
        rhs = jnp.concatenate([bh, bl, bh], axis=1)
        dims = (((1,), (1,)), ((), ()))
    else:
        lhs = jnp.concatenate([ah, ah, al], axis=0)
        rhs = jnp.concatenate([bh, bl, bh], axis=0)
        dims = (((0,), (0,)), ((), ()))
    return lax.dot_general(lhs, rhs, dims, preferred_element_type=F32)


def _dot2_exact_rhs(a, b_bf16):
    ah, al = _hi_lo(a)
    lhs = jnp.concatenate([ah, al], axis=1)
    rhs = jnp.concatenate([b_bf16, b_bf16], axis=0)
    return jnp.dot(lhs, rhs, preferred_element_type=F32)


def _dot2_exact_lhs(a_bf16, b):
    bh, bl = _hi_lo(b)
    lhs = jnp.concatenate([a_bf16, a_bf16], axis=1)
    rhs = jnp.concatenate([bh, bl], axis=0)
    return jnp.dot(lhs, rhs, preferred_element_type=F32)


def _norm_matmul_body(x_ref, g_ref, w_ref, o_ref):
    xn = _rms(x_ref[...], g_ref[...])
    o_ref[...] = jnp.dot(xn.astype(BF16), w_ref[...], preferred_element_type=F32)


def _norm_matmul(x, g, w, tm):
    m, k = x.shape
    n = w.shape[1]
    return pl.pallas_call(
        _norm_matmul_body,
        grid=(m // tm,),
        in_specs=[pl.BlockSpec((tm, k), lambda i: (i, 0)),
                  pl.BlockSpec((1, k), lambda i: (0, 0)),
                  pl.BlockSpec((k, n), lambda i: (0, 0))],
        out_specs=pl.BlockSpec((tm, n), lambda i: (i, 0)),
        out_shape=jax.ShapeDtypeStruct((m, n), F32),
        compiler_params=_params(),
        name="norm_matmul",
    )(x, g, w)


def _sum_matmul_res_body(a_ref, w_ref, r_ref, o_ref):
    o_ref[...] = r_ref[...] + jnp.dot(a_ref[...].astype(BF16), w_ref[...], preferred_element_type=F32)


def _matmul_res(a, w, res, tm):
    m, k = a.shape
    n = w.shape[1]
    return pl.pallas_call(
        _sum_matmul_res_body,
        grid=(m // tm,),
        in_specs=[pl.BlockSpec((tm, k), lambda i: (i, 0)),
                  pl.BlockSpec((k, n), lambda i: (0, 0)),
                  pl.BlockSpec((tm, n), lambda i: (i, 0))],
        out_specs=pl.BlockSpec((tm, n), lambda i: (i, 0)),
        out_shape=jax.ShapeDtypeStruct((m, n), F32),
        compiler_params=_params(),
        name="matmul_res",
    )(a, w, res)


def _gmlp_body(u_ref, v_ref, ga_ref, vg_ref, ws_ref, bs_ref, ya_ref, vn_ref, *, rows):
    gv = _gelu(v_ref[...])
    vn = _rms(gv, vg_ref[...])
    vn_ref[...] = vn
    ri = lax.broadcasted_iota(jnp.int32, (rows, rows), 0)
    ci = lax.broadcasted_iota(jnp.int32, (rows, rows), 1)
    tril = ri >= ci
    gu = _gelu(u_ref[...])
    gate = _sigmoid(ga_ref[...])
    bias = bs_ref[...]
    width = vn.shape[1] // GMLP_GROUPS
    for g in range(GMLP_GROUPS):
        sl = slice(g * width, (g + 1) * width)
        w = jnp.where(tril, ws_ref[g], 0.0)
        sp = _bdot(w, vn[:, sl]) + bias[:, sl]
        ya_ref[:, sl] = gate[:, sl] * (gu[:, sl] * sp)


def _gmlp(uv, gates, vg, ws, bs_full, rows):
    m = uv.shape[0]
    w = uv.shape[1] // 2
    return pl.pallas_call(
        functools.partial(_gmlp_body, rows=rows),
        grid=(m // rows,),
        in_specs=[pl.BlockSpec((rows, w), lambda i: (i, 0)),
                  pl.BlockSpec((rows, w), lambda i: (i, 1)),
                  pl.BlockSpec((rows, w), lambda i: (i, 0)),
                  pl.BlockSpec((1, w), lambda i: (0, 0)),
                  pl.BlockSpec((GMLP_GROUPS, rows, rows), lambda i: (0, 0, 0)),
                  pl.BlockSpec((rows, w), lambda i: (0, 0))],
        out_specs=[pl.BlockSpec((rows, w), lambda i: (i, 0)),
                   pl.BlockSpec((rows, w), lambda i: (i, 0))],
        out_shape=[jax.ShapeDtypeStruct((m, w), F32), jax.ShapeDtypeStruct((m, w), F32)],
        compiler_params=_params(),
        name="gmlp",
    )(uv, uv, gates, vg, ws, bs_full)


def _rwkv_pair(r, kf, v, kkr, a, logd, st2, rk, lnw, lnb, consts):
    m0, strict, incl, bd, eye, bd_ones, tril_ones, n_levels = consts
    chunk = r.shape[0]

    def stack(z):
        return jnp.concatenate([jnp.where(m0, z, 0.0), jnp.where(m0, 0.0, z)], axis=0)

    ss = _dot2_exact_rhs(kkr * kkr, bd_ones)
    kk = kkr / jnp.maximum(jnp.sqrt(ss), 1e-12)
    cl = _dot2_exact_lhs(tril_ones, logd)
    cl_end = cl[chunk - 1:chunk, :]
    d_in = jnp.exp(cl)
    d_inv = jnp.exp(-cl)
    a_t = -(kk * jnp.exp(cl - logd))
    kb = kk * a
    b_t = kb * d_inv
    k_t = kf * d_inv
    r_t = r * d_in
    d_tail = jnp.exp(cl_end - cl)
    d_end = jnp.exp(cl_end)

    g = _dot3(jnp.concatenate([a_t, r_t], axis=0),
              jnp.concatenate([stack(b_t), stack(k_t)], axis=0), mode="nt")
    n_cat = jnp.where(strict, g[:chunk, :LANES], 0.0)
    a_ak = jnp.where(strict, g[:chunk, LANES:], 0.0)
    p_cat = jnp.where(incl, g[chunk:, :LANES], 0.0)
    q_cat = jnp.where(incl, g[chunk:, LANES:], 0.0)

    v_st = stack(v)
    x = _dot3(jnp.concatenate([a_t, a_ak], axis=1), jnp.concatenate([st2, v_st], axis=0))
    n_pow = n_cat
    for lvl in range(n_levels):
        if lvl < n_levels - 1:
            z = _dot3(n_pow, jnp.concatenate([stack(x), stack(n_pow)], axis=1))
            x = x + z[:, :LANES]
            n_pow = z[:, LANES:]
        else:
            x = x + _dot3(n_pow, stack(x))
    u = x

    o = _dot3(jnp.concatenate([r_t, p_cat, q_cat], axis=1),
              jnp.concatenate([st2, stack(u), v_st], axis=0))

    e_diag = jnp.where(eye, jnp.broadcast_to(d_end, (LANES, LANES)), 0.0)
    st_new = _dot3(jnp.concatenate([e_diag, kb * d_tail, kf * d_tail], axis=0),
                   jnp.concatenate([st2, u, v], axis=0), mode="tn")
    st_new = jnp.where(bd, st_new, 0.0)

    inv_n = 1.0 / RWKV_HEAD_DIM
    mu = _dot2_exact_rhs(o, bd_ones) * inv_n
    dlt = o - mu
    var = _dot2_exact_rhs(dlt * dlt, bd_ones) * inv_n
    on = dlt * lax.rsqrt(var + GN_EPS) * lnw + lnb
    bonus = _dot2_exact_rhs(r * kf * rk, bd_ones) * v
    return on + bonus, st_new


def _rwkv_body(rw_ref, gb_ref, ya_ref, sh0_ref, st0_ref, mu_ref, w0_ref, a0_ref, w2_ref, a2_ref, g2_ref,
               kk_ref, ka_ref, rk_ref, lnw_ref, lnb_ref, out_ref, stout_ref, last_ref, st_s, carry_s,
               *, chunk, t_valid, n_chunks):
    c = pl.program_id(1)
    width = out_ref.shape[2]
    n_pairs = width // LANES

    @pl.when(c == 0)
    def _init():
        st_s[...] = st0_ref[0]
        carry_s[...] = sh0_ref[0]

    rw = rw_ref[0]
    cols = rw.shape[1]
    row_w = lax.broadcasted_iota(jnp.int32, (chunk, cols), 0)
    rw_prev = jnp.where(row_w == 0, carry_s[0:1, :], pltpu.roll(rw, 1, axis=0))
    last_row = rw[(t_valid - 1) % chunk:(t_valid - 1) % chunk + 1, :]
    carry_s[0:1, :] = last_row
    last_ref[0] = jnp.broadcast_to(last_row, (8, cols))
    mix = rw + (rw_prev - rw) * mu_ref[...]

    r = mix[:, 0:width]
    k = mix[:, width:2 * width]
    v = mix[:, 2 * width:3 * width]
    wa = mix[:, 3 * width:3 * width + LANES]
    gl = mix[:, 3 * width + LANES:3 * width + 2 * LANES]
    z = -(w0_ref[...] + _bdot(jnp.tanh(wa), w2_ref[...]))
    w = -(jnp.maximum(z, 0.0) + jnp.log1p(jnp.exp(-jnp.abs(z)))) - 0.5
    logd = -jnp.exp(w)
    a = _sigmoid(a0_ref[...] + _bdot(wa, a2_ref[...]))
    gate_g = _bdot(_sigmoid(gl), g2_ref[...])
    kkr = k * kk_ref[...]
    kf = k * (1.0 + (a - 1.0) * ka_ref[...])
    if t_valid < chunk * n_chunks:
        valid = lax.broadcasted_iota(jnp.int32, (chunk, width), 0) + c * chunk < t_valid
        logd = jnp.where(valid, logd, 0.0)
        kkr = jnp.where(valid, kkr, 0.0)
        kf = jnp.where(valid, kf, 0.0)
        v = jnp.where(valid, v, 0.0)

    lane = lax.broadcasted_iota(jnp.int32, (chunk, LANES), 1)
    rowi = lax.broadcasted_iota(jnp.int32, (chunk, LANES), 0)
    m0 = lane < RWKV_HEAD_DIM
    lane_in_head = jnp.where(m0, lane, lane - RWKV_HEAD_DIM)
    r2 = lax.broadcasted_iota(jnp.int32, (LANES, LANES), 0)
    c2 = lax.broadcasted_iota(jnp.int32, (LANES, LANES), 1)
    bd = (r2 < RWKV_HEAD_DIM) == (c2 < RWKV_HEAD_DIM)
    rl = lax.broadcasted_iota(jnp.int32, (chunk, chunk), 0)
    cl_ = lax.broadcasted_iota(jnp.int32, (chunk, chunk), 1)
    consts = (m0, rowi > lane_in_head, rowi >= lane_in_head, bd, r2 == c2,
              jnp.where(bd, 1.0, 0.0).astype(BF16), jnp.where(rl >= cl_, 1.0, 0.0).astype(BF16),
              chunk.bit_length() - 1)

    gate_b = _sigmoid(gb_ref[0])
    ya = ya_ref[0]
    for p in range(n_pairs):
        sl = slice(p * LANES, (p + 1) * LANES)
        y, st_new = _rwkv_pair(r[:, sl], kf[:, sl], v[:, sl], kkr[:, sl], a[:, sl], logd[:, sl], st_s[p],
                               rk_ref[:, sl], lnw_ref[:, sl], lnb_ref[:, sl], consts)
        st_s[p] = st_new
        out_ref[0, :, sl] = ya[:, sl] + gate_b[:, sl] * (y * gate_g[:, sl])

    @pl.when(c == n_chunks - 1)
    def _fin():
        stout_ref[0] = st_s[...]


def _rwkv(rw, gates, ya, shift0, st0, p, t_valid):
    b, tp, cols = rw.shape
    width = ya.shape[2]
    n_pairs = width // LANES
    chunk = RWKV_CHUNK
    n_chunks = tp // chunk
    vec = lambda n: pl.BlockSpec((1, n), lambda i, j: (0, 0))
    lora = lambda n: pl.BlockSpec((n, width), lambda i, j: (0, 0))
    return pl.pallas_call(
        functools.partial(_rwkv_body, chunk=chunk, t_valid=t_valid, n_chunks=n_chunks),
        grid=(b, n_chunks),
        in_specs=[pl.BlockSpec((1, chunk, cols), lambda i, j: (i, j, 0)),
                  pl.BlockSpec((1, chunk, width), lambda i, j: (i, j, 1)),
                  pl.BlockSpec((1, chunk, width), lambda i, j: (i, j, 0)),
                  pl.BlockSpec((1, 8, cols), lambda i, j: (i, 0, 0)),
                  pl.BlockSpec((1, n_pairs, LANES, LANES), lambda i, j: (i, 0, 0, 0)),
                  vec(cols), vec(width), vec(width), lora(LANES), lora(LANES), lora(LANES),
                  vec(width), vec(width), vec(width), vec(width), vec(width)],
        out_specs=[pl.BlockSpec((1, chunk, width), lambda i, j: (i, j, 0)),
                   pl.BlockSpec((1, n_pairs, LANES, LANES), lambda i, j: (i, 0, 0, 0)),
                   pl.BlockSpec((1, 8, cols), lambda i, j: (i, 0, 0))],
        out_shape=[jax.ShapeDtypeStruct((b, tp, width), F32),
                   jax.ShapeDtypeStruct((b, n_pairs, LANES, LANES), F32),
                   jax.ShapeDtypeStruct((b, 8, cols), F32)],
        scratch_shapes=[pltpu.VMEM((n_pairs, LANES, LANES), F32), pltpu.VMEM((8, cols), F32)],
        compiler_params=_params(2),
        name="rwkv7",
    )(rw, gates, ya, shift0, st0, p["mu"], p["w0"], p["a0"], p["w2"], p["a2"], p["g2"],
      p["k_k"], p["k_a"], p["r_k"], p["lnx_w"], p["lnx_b"])


def _attn_body(x_ref, g_ref, wq_ref, mk_ref, mv_ref, wo_ref, o_ref):
    x = x_ref[0]
    q = jnp.dot(_rms(x, g_ref[...]).astype(BF16), wq_ref[...], preferred_element_type=F32)
    mk = mk_ref[0]
    mv = mv_ref[0]
    hd = x.shape[1] // CA_HEADS
    outs = []
    for h in range(CA_HEADS):
        sl = slice(h * hd, (h + 1) * hd)
        s = lax.dot_general(q[:, sl].astype(BF16), mk[:, sl], (((1,), (1,)), ((), ())),
                            preferred_element_type=F32) * (hd ** -0.5)
        e = jnp.exp(s - jnp.max(s, axis=-1, keepdims=True))
        pr = e / jnp.sum(e, axis=-1, keepdims=True)
        outs.append(jnp.dot(pr.astype(BF16), mv[:, sl], preferred_element_type=F32))
    o = jnp.concatenate(outs, axis=-1)
    o_ref[0] = x + jnp.dot(o.astype(BF16), wo_ref[...], preferred_element_type=F32)


def _cross_attn(x, g, wq, mk, mv, wo, tm):
    b, t, d = x.shape
    n_mem = mk.shape[1]
    return pl.pallas_call(
        _attn_body,
        grid=(b, t // tm),
        in_specs=[pl.BlockSpec((1, tm, d), lambda i, j: (i, j, 0)),
                  pl.BlockSpec((1, d), lambda i, j: (0, 0)),
                  pl.BlockSpec((d, d), lambda i, j: (0, 0)),
                  pl.BlockSpec((1, n_mem, d), lambda i, j: (i, 0, 0)),
                  pl.BlockSpec((1, n_mem, d), lambda i, j: (i, 0, 0)),
                  pl.BlockSpec((d, d), lambda i, j: (0, 0))],
        out_specs=pl.BlockSpec((1, tm, d), lambda i, j: (i, j, 0)),
        out_shape=jax.ShapeDtypeStruct((b, t, d), F32),
        compiler_params=_params(2),
        name="cross_attn",
    )(x, g, wq, mk, mv, wo)


def _router_body(x_ref, g_ref, wr_ref, br_ref, h_ref, gate_ref, idx_ref):
    h = _rms(x_ref[...], g_ref[...])
    h_ref[...] = h
    logits = _dot3(h, wr_ref[...]) + br_ref[...]
    lane = lax.broadcasted_iota(jnp.int32, logits.shape, 1)
    neg = jnp.float32(-jnp.inf)
    vals, idxs = [], []
    for _ in range(TOP_K):
        m = jnp.max(logits, axis=-1, keepdims=True)
        idx = jnp.min(jnp.where(logits == m, lane, LANES), axis=-1, keepdims=True)
        vals.append(m)
        idxs.append(idx)
        logits = jnp.where(lane == idx, neg, logits)
    es = [jnp.exp(vv - vals[0]) for vv in vals]
    den = es[0] + es[1] + es[2] + es[3]
    gate = jnp.zeros(logits.shape, F32)
    sel = jnp.zeros(logits.shape, jnp.int32)
    for j in range(TOP_K):
        gate = jnp.where(lane == j, es[j] / den, gate)
        sel = jnp.where(lane == j, idxs[j], sel)
    gate_ref[...] = gate
    idx_ref[...] = sel


def _router(x, g, wr_pad, br_pad, tm):
    m, d = x.shape
    return pl.pallas_call(
        _router_body,
        grid=(m // tm,),
        in_specs=[pl.BlockSpec((tm, d), lambda i: (i, 0)),
                  pl.BlockSpec((1, d), lambda i: (0, 0)),
                  pl.BlockSpec((d, LANES), lambda i: (0, 0)),
                  pl.BlockSpec((1, LANES), lambda i: (0, 0))],
        out_specs=[pl.BlockSpec((tm, d), lambda i: (i, 0)),
                   pl.BlockSpec((tm, LANES), lambda i: (i, 0)),
                   pl.BlockSpec((tm, LANES), lambda i: (i, 0))],
        out_shape=[jax.ShapeDtypeStruct((m, d), F32),
                   jax.ShapeDtypeStruct((m, LANES), F32),
                   jax.ShapeDtypeStruct((m, LANES), jnp.int32)],
        compiler_params=_params(),
        name="moe_router",
    )(x, g, wr_pad, br_pad)


def _row_copy_wait(src_row, dst_row, sem, count):
    def body(_, carry):
        pltpu.make_async_copy(src_row, dst_row, sem).wait()
        return carry
    lax.fori_loop(0, count, body, 0)


def _dispatch_body(dest_ref, h_ref, init_ref, xs_ref, sem, *, tm):
    del init_ref

    def issue(i, carry):
        for k in range(TOP_K):
            pltpu.make_async_copy(h_ref.at[pl.ds(i, 1)], xs_ref.at[pl.ds(dest_ref[0, k, i], 1)], sem).start()
        return carry
    lax.fori_loop(0, tm, issue, 0)
    _row_copy_wait(h_ref.at[pl.ds(0, 1)], xs_ref.at[pl.ds(0, 1)], sem, tm * TOP_K)


def _dispatch(h, dest_tiles, n_rows, tm):
    m, d = h.shape
    xs0 = jnp.zeros((n_rows, d), F32)
    return pl.pallas_call(
        functools.partial(_dispatch_body, tm=tm),
        grid=(m // tm,),
        in_specs=[pl.BlockSpec((1, TOP_K, tm), lambda i: (i, 0, 0), memory_space=pltpu.SMEM),
                  pl.BlockSpec((tm, d), lambda i: (i, 0)),
                  pl.BlockSpec(memory_space=pl.ANY)],
        out_specs=pl.BlockSpec(memory_space=pl.ANY),
        out_shape=jax.ShapeDtypeStruct((n_rows, d), F32),
        scratch_shapes=[pltpu.SemaphoreType.DMA(())],
        input_output_aliases={2: 0},
        compiler_params=_params(),
        name="moe_dispatch",
    )(dest_tiles, h, xs0)


def _expert_body(blk_e_ref, n_used_ref, xs_ref, wgu_ref, bgu_ref, wdn_ref, bdn_ref, ys_ref):
    del blk_e_ref
    i = pl.program_id(0)
    f = wdn_ref.shape[1]

    @pl.when(i < n_used_ref[0])
    def _run():
        hh = jnp.dot(xs_ref[...].astype(BF16), wgu_ref[0], preferred_element_type=F32) + bgu_ref[0]
        hg = jnp.minimum(hh[:, :f], SWIGLU_LIMIT)
        hl = jnp.clip(hh[:, f:], -SWIGLU_LIMIT, SWIGLU_LIMIT)
        act = hg * _sigmoid(SWIGLU_ALPHA * hg) * (hl + 1.0)
        ys_ref[...] = jnp.dot(act.astype(BF16), wdn_ref[0], preferred_element_type=F32) + bdn_ref[0]

    @pl.when(i >= n_used_ref[0])
    def _skip():
        ys_ref[...] = jnp.zeros(ys_ref.shape, F32)


def _experts(xs, blk_e, n_used, wgu, bgu, wdn, bdn):
    n_rows, d = xs.shape
    f = wdn.shape[1]
    n_blk = n_rows // MOE_BLOCK
    grid_spec = pltpu.PrefetchScalarGridSpec(
        num_scalar_prefetch=2,
        grid=(n_blk,),
        in_specs=[pl.BlockSpec((MOE_BLOCK, d), lambda i, be, nu: (i, 0)),
                  pl.BlockSpec((1, d, 2 * f), lambda i, be, nu: (be[i], 0, 0)),
                  pl.BlockSpec((1, 1, 2 * f), lambda i, be, nu: (be[i], 0, 0)),
                  pl.BlockSpec((1, f, d), lambda i, be, nu: (be[i], 0, 0)),
                  pl.BlockSpec((1, 1, d), lambda i, be, nu: (be[i], 0, 0))],
        out_specs=pl.BlockSpec((MOE_BLOCK, d), lambda i, be, nu: (i, 0)),
    )
    return pl.pallas_call(
        _expert_body,
        grid_spec=grid_spec,
        out_shape=jax.ShapeDtypeStruct((n_rows, d), F32),
        compiler_params=_params(),
        name="moe_experts",
    )(blk_e, n_used, xs, wgu, bgu, wdn, bdn)


def _combine_body(dest_ref, gate_ref, x_ref, fg_ref, ys_ref, y_ref, buf, sem, *, tm):
    def issue(i, carry):
        for k in range(TOP_K):
            pltpu.make_async_copy(ys_ref.at[pl.ds(dest_ref[0, k, i], 1)], buf.at[k, pl.ds(i, 1)], sem).start()
        return carry
    lax.fori_loop(0, tm, issue, 0)
    _row_copy_wait(ys_ref.at[pl.ds(0, 1)], buf.at[0, pl.ds(0, 1)], sem, tm * TOP_K)
    acc = x_ref[...]
    gate = gate_ref[...]
    for k in range(TOP_K):
        acc = acc + gate[:, k:k + 1] * buf[k]
    y_ref[...] = _rms(acc, fg_ref[...])


def _combine(dest_tiles, gate, x, fg, ys, tm):
    m, d = x.shape
    return pl.pallas_call(
        functools.partial(_combine_body, tm=tm),
        grid=(m // tm,),
        in_specs=[pl.BlockSpec((1, TOP_K, tm), lambda i: (i, 0, 0), memory_space=pltpu.SMEM),
                  pl.BlockSpec((tm, LANES), lambda i: (i, 0)),
                  pl.BlockSpec((tm, d), lambda i: (i, 0)),
                  pl.BlockSpec((1, d), lambda i: (0, 0)),
                  pl.BlockSpec(memory_space=pl.ANY)],
        out_specs=pl.BlockSpec((tm, d), lambda i: (i, 0)),
        out_shape=jax.ShapeDtypeStruct((m, d), F32),
        scratch_shapes=[pltpu.VMEM((TOP_K, tm, d), F32), pltpu.SemaphoreType.DMA(())],
        compiler_params=_params(),
        name="moe_combine",
    )(dest_tiles, gate, x, fg, ys)


def _moe_final(x2d, w, tm):
    m, d = x2d.shape
    h, gate, sel = _router(x2d, w["norm_ffn_g"], w["router_w"], w["router_b"], tm)
    flat_e = sel[:, :TOP_K].reshape(-1)
    onehot = (flat_e[:, None] == jnp.arange(N_EXPERTS, dtype=jnp.int32)[None, :]).astype(jnp.int32)
    csum = jnp.cumsum(onehot, axis=0)
    rank = jnp.sum(onehot * csum, axis=1) - 1
    counts = csum[-1]
    padded = (counts + MOE_BLOCK - 1) // MOE_BLOCK * MOE_BLOCK
    pad_end = jnp.cumsum(padded)
    pad_start = pad_end - padded
    dest = jnp.sum(onehot * pad_start[None, :], axis=1) + rank
    n_assign = m * TOP_K
    n_blk = -(-(n_assign + N_EXPERTS * (MOE_BLOCK - 1)) // MOE_BLOCK)
    n_rows = n_blk * MOE_BLOCK
    blk_e = jnp.minimum(jnp.searchsorted(pad_end, jnp.arange(n_blk, dtype=jnp.int32) * MOE_BLOCK, side="right"),
                        N_EXPERTS - 1).astype(jnp.int32)
    n_used = (pad_end[-1:] // MOE_BLOCK).astype(jnp.int32)
    dest_tiles = dest.astype(jnp.int32).reshape(m // tm, tm, TOP_K).transpose(0, 2, 1)
    xs = _dispatch(h, dest_tiles, n_rows, tm)
    ys = _experts(xs, blk_e, n_used, w["moe_w_gu"], w["moe_b_gu"], w["moe_w_down"], w["moe_b_down"])
    return _combine(dest_tiles, gate, x2d, w["final_norm_g"], ys, tm)


def _layer(x, shift_prev, s0, mk, mv, w):
    b, t, d = x.shape
    m = b * t
    tm = min(256, m)
    x2d = x.reshape(m, d)
    uv = _norm_matmul(x2d, w["norm_mix_g"], w["w_uv"], tm)
    rw = _norm_matmul(x2d, w["norm_mix_g"], w["w_rw"], tm)
    gates = _norm_matmul(x2d, w["norm_mix_g"], w["w_gates"], tm)
    cols = rw.shape[1]

    rows = min(t, GMLP_CHUNK)
    ws = w["gmlp_ws"][:, :rows, :rows]
    bs_full = jnp.repeat(w["gmlp_bs"][:, :rows].T, d // GMLP_GROUPS, axis=1)
    ya, v_n = _gmlp(uv, gates, w["gmlp_v_norm_g"], ws, bs_full, rows)

    n_heads = s0.shape[1]
    n_pairs = n_heads // 2
    tp = -(-t // RWKV_CHUNK) * RWKV_CHUNK
    pad3 = lambda z: jnp.pad(z.reshape(b, t, -1), ((0, 0), (0, tp - t), (0, 0)))
    st_t = jnp.swapaxes(s0, 2, 3).reshape(b, n_pairs, 2, RWKV_HEAD_DIM, RWKV_HEAD_DIM)
    st0 = jnp.zeros((b, n_pairs, LANES, LANES), F32)
    st0 = st0.at[:, :, :RWKV_HEAD_DIM, :RWKV_HEAD_DIM].set(st_t[:, :, 0])
    st0 = st0.at[:, :, RWKV_HEAD_DIM:, RWKV_HEAD_DIM:].set(st_t[:, :, 1])
    shift0 = jnp.broadcast_to(shift_prev, (b, 8, cols))
    merged, st_out, last = _rwkv(pad3(rw), pad3(gates), pad3(ya), shift0, st0, w, t)
    merged = merged[:, :t].reshape(m, d)
    s_new = jnp.stack([st_out[:, :, :RWKV_HEAD_DIM, :RWKV_HEAD_DIM], st_out[:, :, RWKV_HEAD_DIM:, RWKV_HEAD_DIM:]],
                      axis=2).reshape(b, n_heads, RWKV_HEAD_DIM, RWKV_HEAD_DIM)
    s_new = jnp.swapaxes(s_new, 2, 3)
    shift_new = last[:, :1, :]

    x1 = _matmul_res(merged, w["w_out"], x2d, tm)
    x2 = _cross_attn(x1.reshape(b, t, d), w["norm_ca_g"], w["ca_wq"], mk, mv, w["ca_wo"], min(512, t))
    y = _moe_final(x2.reshape(m, d), w, min(256, m))
    return y.reshape(b, t, d), shift_new, s_new, v_n.reshape(b, t, d)


def kernel(x_prompt, x_sample, mem_prompt, state_shift, state_rwkv, cache_mem_k, cache_mem_v, norm_mix_g, w_in, gmlp_v_norm_g, gmlp_ws, gmlp_bs, rwkv_mu, rwkv_w0, rwkv_w2, rwkv_a0, rwkv_a2, rwkv_g2, rwkv_k_k, rwkv_k_a, rwkv_r_k, rwkv_lnx_w, rwkv_lnx_b, w_out, norm_ca_g, norm_mem_g, ca_wq, ca_wk, ca_wv, ca_wo, norm_ffn_g, router_w, router_b, moe_w_gu, moe_b_gu, moe_w_down, moe_b_down, final_norm_g):
    depth = w_in.shape[0]
    assert depth == 1, "the fused MoE + final-norm tail assumes a single layer"
    l = 0
    bp, _, d = x_prompt.shape
    width = rwkv_w0.shape[1]
    n_heads = width // RWKV_HEAD_DIM
    gw = gmlp_v_norm_g.shape[1]
    row = lambda z: z.reshape(1, -1)
    zpad = jnp.zeros((LANES - rwkv_w2.shape[1], width), F32)
    w = dict(
        norm_mix_g=row(norm_mix_g[l]),
        w_uv=w_in[l][:, :2 * gw].astype(BF16),
        w_rw=w_in[l][:, 2 * gw:w_in.shape[2] - 2 * d].astype(BF16),
        w_gates=w_in[l][:, w_in.shape[2] - 2 * d:].astype(BF16),
        gmlp_v_norm_g=row(gmlp_v_norm_g[l]), gmlp_ws=gmlp_ws[l], gmlp_bs=gmlp_bs[l],
        mu=row(rwkv_mu[l]), w0=row(rwkv_w0[l]), a0=row(rwkv_a0[l]),
        w2=jnp.concatenate([rwkv_w2[l], zpad], axis=0).astype(BF16),
        a2=jnp.concatenate([zpad, rwkv_a2[l]], axis=0).astype(BF16),
        g2=rwkv_g2[l].astype(BF16),
        k_k=row(rwkv_k_k[l]), k_a=row(rwkv_k_a[l]), r_k=row(rwkv_r_k[l]),
        lnx_w=row(rwkv_lnx_w[l]), lnx_b=row(rwkv_lnx_b[l]),
        w_out=w_out[l].astype(BF16), norm_ca_g=row(norm_ca_g[l]),
        ca_wq=ca_wq[l].astype(BF16), ca_wo=ca_wo[l].astype(BF16),
        norm_ffn_g=row(norm_ffn_g[l]),
        router_w=jnp.pad(router_w[l], ((0, 0), (0, LANES - N_EXPERTS))),
        router_b=jnp.pad(row(router_b[l]), ((0, 0), (0, LANES - N_EXPERTS)), constant_values=-jnp.inf),
        moe_w_gu=moe_w_gu[l].astype(BF16), moe_b_gu=moe_b_gu[l][:, None, :],
        moe_w_down=moe_w_down[l].astype(BF16), moe_b_down=moe_b_down[l][:, None, :],
        final_norm_g=row(final_norm_g),
    )

    n_mem = mem_prompt.shape[1]
    mem2d = mem_prompt.reshape(bp * n_mem, d)
    mk = _norm_matmul(mem2d, row(norm_mem_g[l]), ca_wk[l].astype(BF16), 256).reshape(bp, n_mem, d)
    mv = _norm_matmul(mem2d, row(norm_mem_g[l]), ca_wv[l].astype(BF16), 256).reshape(bp, n_mem, d)
    shift0 = jnp.zeros((bp, 1, rwkv_mu.shape[1]), F32)
    s_zero = jnp.zeros((bp, n_heads, RWKV_HEAD_DIM, RWKV_HEAD_DIM), F32)
    y_p, sh_p, st_p, _ = _layer(x_prompt, shift0, s_zero, mk.astype(BF16), mv.astype(BF16), w)

    bs = x_sample.shape[0]
    cmk = cache_mem_k[l].reshape(bs, n_mem, d).astype(BF16)
    cmv = cache_mem_v[l].reshape(bs, n_mem, d).astype(BF16)
    y_s, sh_s, st_s, v_s = _layer(x_sample, state_shift[l], state_rwkv[l], cmk, cmv, w)

    ca_shape = (1, bp, n_mem, CA_HEADS, d // CA_HEADS)
    return (y_p, y_s, sh_p[None], st_p[None], mk.reshape(ca_shape), mv.reshape(ca_shape),
            sh_s[None], st_s[None], v_s[None])
```

```python
import functools

import jax
import jax.numpy as jnp
from jax import lax
from jax.experimental import pallas as pl
from jax.experimental.pallas import tpu as pltpu

F32 = jnp.float32
BF16 = jnp.bfloat16

NORM_EPS = 1e-5
GN_EPS = 64e-5
RWKV_HEAD_DIM = 64
RWKV_CHUNK = 64
GMLP_CHUNK = 128
GMLP_GROUPS = 8
CA_HEADS = 4
N_EXPERTS = 32
TOP_K = 4
SWIGLU_LIMIT = 7.0
SWIGLU_ALPHA = 1.702
LANES = 128
MOE_BLOCK = 256
ROW_UNROLL = 8
VMEM_LIMIT = 56 * 1024 * 1024


def _params(n_axes=1):
    return pltpu.CompilerParams(dimension_semantics=("arbitrary",) * n_axes, vmem_limit_bytes=VMEM_LIMIT)


def _rms(x, g):
    return x * lax.rsqrt(jnp.mean(x * x, axis=-1, keepdims=True) + NORM_EPS) * g


def _gelu(x):
    return x * (lax.erf(x * 0.7071067811865476) + 1.0) * 0.5


def _sigmoid(x):
    return 1.0 / (1.0 + jnp.exp(-x))


def _bdot(a, b):
    return jnp.dot(a.astype(BF16), b.astype(BF16), preferred_element_type=F32)


def _hi_lo(x):
    hi = x.astype(BF16)
    lo = (x - hi.astype(F32)).astype(BF16)
    return hi, lo


def _dot3(a, b, mode="nn"):
    ah, al = _hi_lo(a)
    bh, bl = _hi_lo(b)
    if mode == "nn":
        lhs = jnp.concatenate([ah, ah, al], axis=1)
        rhs = jnp.concatenate([bh, bl, bh], axis=0)
        dims = (((1,), (0,)), ((), ()))
    elif mode == "nt":
        lhs = jnp.concatenate([ah, ah, al], axis=1)
        rhs = jnp.concatenate([bh, bl, bh], axis=1)
        dims = (((1,), (1,)), ((), ()))
    else:
        lhs = jnp.concatenate([ah, ah, al], axis=0)
        rhs = jnp.concatenate([bh, bl, bh], axis=0)
        dims = (((0,), (0,)), ((), ()))
    return lax.dot_general(lhs, rhs, dims, preferred_element_type=F32)


def _dot2_exact_rhs(a, b_bf16):
    ah, al = _hi_lo(a)
    lhs = jnp.concatenate([ah, al], axis=1)
    rhs = jnp.concatenate([b_bf16, b_bf16], axis=0)
    return jnp.dot(lhs, rhs, preferred_element_type=F32)


def _dot2_exact_lhs(a_bf16, b):
    bh, bl = _hi_lo(b)
    lhs = jnp.concatenate([a_bf16, a_bf16], axis=1)
    rhs = jnp.concatenate([bh, bl], axis=0)
    return jnp.dot(lhs, rhs, preferred_element_type=F32)


def _norm_matmul_body(x_ref, g_ref, w_ref, o_ref):
    xn = _rms(x_ref[...], g_ref[...])
    o_ref[...] = jnp.dot(xn.astype(BF16), w_ref[...], preferred_element_type=F32)


def _norm_matmul(x, g, w, tm):
    m, k = x.shape
    n = w.shape[1]
    return pl.pallas_call(
        _norm_matmul_body,
        grid=(m // tm,),
        in_specs=[pl.BlockSpec((tm, k), lambda i: (i, 0)),
                  pl.BlockSpec((1, k), lambda i: (0, 0)),
                  pl.BlockSpec((k, n), lambda i: (0, 0))],
        out_specs=pl.BlockSpec((tm, n), lambda i: (i, 0)),
        out_shape=jax.ShapeDtypeStruct((m, n), F32),
        compiler_params=_params(),
        name="norm_matmul",
    )(x, g, w)


def _sum_matmul_res_body(a_ref, w_ref, r_ref, o_ref):
    o_ref[...] = r_ref[...] + jnp.dot(a_ref[...].astype(BF16), w_ref[...], preferred_element_type=F32)


def _matmul_res(a, w, res, tm):
    m, k = a.shape
    n = w.shape[1]
    return pl.pallas_call(
        _sum_matmul_res_body,
        grid=(m // tm,),
        in_specs=[pl.BlockSpec((tm, k), lambda i: (i, 0)),
                  pl.BlockSpec((k, n), lambda i: (0, 0)),
                  pl.BlockSpec((tm, n), lambda i: (i, 0))],
        out_specs=pl.BlockSpec((tm, n), lambda i: (i, 0)),
        out_shape=jax.ShapeDtypeStruct((m, n), F32),
        compiler_params=_params(),
        name="matmul_res",
    )(a, w, res)


def _gmlp_body(u_ref, v_ref, ga_ref, vg_ref, ws_ref, bs_ref, ya_ref, vn_ref, *, rows):
    gv = _gelu(v_ref[...])
    vn = _rms(gv, vg_ref[...])
    vn_ref[...] = vn
    ri = lax.broadcasted_iota(jnp.int32, (rows, rows), 0)
    ci = lax.broadcasted_iota(jnp.int32, (rows, rows), 1)
    tril = ri >= ci
    gu = _gelu(u_ref[...])
    gate = _sigmoid(ga_ref[...])
    bias = bs_ref[...]
    width = vn.shape[1] // GMLP_GROUPS
    for g in range(GMLP_GROUPS):
        sl = slice(g * width, (g + 1) * width)
        w = jnp.where(tril, ws_ref[g], 0.0)
        sp = _bdot(w, vn[:, sl]) + bias[:, sl]
        ya_ref[:, sl] = gate[:, sl] * (gu[:, sl] * sp)


def _gmlp(uv, gates, vg, ws, bs_full, rows):
    m = uv.shape[0]
    w = uv.shape[1] // 2
    return pl.pallas_call(
        functools.partial(_gmlp_body, rows=rows),
        grid=(m // rows,),
        in_specs=[pl.BlockSpec((rows, w), lambda i: (i, 0)),
                  pl.BlockSpec((rows, w), lambda i: (i, 1)),
                  pl.BlockSpec((rows, w), lambda i: (i, 0)),
                  pl.BlockSpec((1, w), lambda i: (0, 0)),
                  pl.BlockSpec((GMLP_GROUPS, rows, rows), lambda i: (0, 0, 0)),
                  pl.BlockSpec((rows, w), lambda i: (0, 0))],
        out_specs=[pl.BlockSpec((rows, w), lambda i: (i, 0)),
                   pl.BlockSpec((rows, w), lambda i: (i, 0))],
        out_shape=[jax.ShapeDtypeStruct((m, w), F32), jax.ShapeDtypeStruct((m, w), F32)],
        compiler_params=_params(),
        name="gmlp",
    )(uv, uv, gates, vg, ws, bs_full)


def _rwkv_pair(r, kf, v, kkr, a, logd, st2, rk, lnw, lnb, consts):
    m0, strict, incl, bd, eye, bd_ones, tril_ones, n_levels = consts
    chunk = r.shape[0]

    def stack(z):
        return jnp.concatenate([jnp.where(m0, z, 0.0), jnp.where(m0, 0.0, z)], axis=0)

    ss = _dot2_exact_rhs(kkr * kkr, bd_ones)
    cl = _dot2_exact_lhs(tril_ones, logd)
    bonus = _dot2_exact_rhs(r * kf * rk, bd_ones) * v
    yield
    kk = kkr / jnp.maximum(jnp.sqrt(ss), 1e-12)
    cl_end = cl[chunk - 1:chunk, :]
    d_in = jnp.exp(cl)
    d_inv = jnp.exp(-cl)
    a_t = -(kk * jnp.exp(cl - logd))
    kb = kk * a
    b_t = kb * d_inv
    k_t = kf * d_inv
    r_t = r * d_in
    d_tail = jnp.exp(cl_end - cl)
    d_end = jnp.exp(cl_end)

    g = _dot3(jnp.concatenate([a_t, r_t], axis=0),
              jnp.concatenate([stack(b_t), stack(k_t)], axis=0), mode="nt")
    yield
    n_cat = jnp.where(strict, g[:chunk, :LANES], 0.0)
    a_ak = jnp.where(strict, g[:chunk, LANES:], 0.0)
    p_cat = jnp.where(incl, g[chunk:, :LANES], 0.0)
    q_cat = jnp.where(incl, g[chunk:, LANES:], 0.0)

    v_st = stack(v)
    x = _dot3(jnp.concatenate([a_t, a_ak], axis=1), jnp.concatenate([st2, v_st], axis=0))
    yield
    n_pow = n_cat
    for lvl in range(n_levels):
        if lvl < n_levels - 1:
            z = _dot3(n_pow, jnp.concatenate([stack(x), stack(n_pow)], axis=1))
            yield
            x = x + z[:, :LANES]
            n_pow = z[:, LANES:]
        else:
            z = _dot3(n_pow, stack(x))
            yield
            x = x + z
    u = x

    o = _dot3(jnp.concatenate([r_t, p_cat, q_cat], axis=1),
              jnp.concatenate([st2, stack(u), v_st], axis=0))
    e_diag = jnp.where(eye, jnp.broadcast_to(d_end, (LANES, LANES)), 0.0)
    st_new = _dot3(jnp.concatenate([e_diag, kb * d_tail, kf * d_tail], axis=0),
                   jnp.concatenate([st2, u, v], axis=0), mode="tn")
    yield
    st_new = jnp.where(bd, st_new, 0.0)

    inv_n = 1.0 / RWKV_HEAD_DIM
    mu = _dot2_exact_rhs(o, bd_ones) * inv_n
    yield
    dlt = o - mu
    var = _dot2_exact_rhs(dlt * dlt, bd_ones) * inv_n
    yield
    on = dlt * lax.rsqrt(var + GN_EPS) * lnw + lnb
    return on + bonus, st_new


def _interleave(generators):
    results = [None] * len(generators)
    live = list(range(len(generators)))
    while live:
        still = []
        for i in live:
            try:
                next(generators[i])
                still.append(i)
            except StopIteration as stop:
                results[i] = stop.value
        live = still
    return results


def _rwkv_body(rw_ref, gb_ref, ya_ref, sh0_ref, st0_ref, mu_ref, w0_ref, a0_ref, w2_ref, a2_ref, g2_ref,
               kk_ref, ka_ref, rk_ref, lnw_ref, lnb_ref, out_ref, stout_ref, last_ref, st_s, carry_s,
               *, chunk, t_valid, n_chunks):
    c = pl.program_id(1)
    width = out_ref.shape[2]
    n_pairs = width // LANES

    @pl.when(c == 0)
    def _init():
        st_s[...] = st0_ref[0]
        carry_s[...] = sh0_ref[0]

    rw = rw_ref[0]
    cols = rw.shape[1]
    row_w = lax.broadcasted_iota(jnp.int32, (chunk, cols), 0)
    rw_prev = jnp.where(row_w == 0, carry_s[0:1, :], pltpu.roll(rw, 1, axis=0))
    last_row = rw[(t_valid - 1) % chunk:(t_valid - 1) % chunk + 1, :]
    carry_s[0:1, :] = last_row
    last_ref[0] = jnp.broadcast_to(last_row, (8, cols))
    mix = rw + (rw_prev - rw) * mu_ref[...]

    r = mix[:, 0:width]
    k = mix[:, width:2 * width]
    v = mix[:, 2 * width:3 * width]
    wa = mix[:, 3 * width:3 * width + LANES]
    gl = mix[:, 3 * width + LANES:3 * width + 2 * LANES]
    z = -(w0_ref[...] + _bdot(jnp.tanh(wa), w2_ref[...]))
    w = -(jnp.maximum(z, 0.0) + jnp.log1p(jnp.exp(-jnp.abs(z)))) - 0.5
    logd = -jnp.exp(w)
    a = _sigmoid(a0_ref[...] + _bdot(wa, a2_ref[...]))
    gate_g = _bdot(_sigmoid(gl), g2_ref[...])
    kkr = k * kk_ref[...]
    kf = k * (1.0 + (a - 1.0) * ka_ref[...])
    if t_valid < chunk * n_chunks:
        valid = lax.broadcasted_iota(jnp.int32, (chunk, width), 0) + c * chunk < t_valid
        logd = jnp.where(valid, logd, 0.0)
        kkr = jnp.where(valid, kkr, 0.0)
        kf = jnp.where(valid, kf, 0.0)
        v = jnp.where(valid, v, 0.0)

    lane = lax.broadcasted_iota(jnp.int32, (chunk, LANES), 1)
    rowi = lax.broadcasted_iota(jnp.int32, (chunk, LANES), 0)
    m0 = lane < RWKV_HEAD_DIM
    lane_in_head = jnp.where(m0, lane, lane - RWKV_HEAD_DIM)
    r2 = lax.broadcasted_iota(jnp.int32, (LANES, LANES), 0)
    c2 = lax.broadcasted_iota(jnp.int32, (LANES, LANES), 1)
    bd = (r2 < RWKV_HEAD_DIM) == (c2 < RWKV_HEAD_DIM)
    rl = lax.broadcasted_iota(jnp.int32, (chunk, chunk), 0)
    cl_ = lax.broadcasted_iota(jnp.int32, (chunk, chunk), 1)
    consts = (m0, rowi > lane_in_head, rowi >= lane_in_head, bd, r2 == c2,
              jnp.where(bd, 1.0, 0.0).astype(BF16), jnp.where(rl >= cl_, 1.0, 0.0).astype(BF16),
              chunk.bit_length() - 1)

    gate_b = _sigmoid(gb_ref[0])
    ya = ya_ref[0]
    rk = rk_ref[...]
    lnw = lnw_ref[...]
    lnb = lnb_ref[...]
    gens = []
    for p in range(n_pairs):
        sl = slice(p * LANES, (p + 1) * LANES)
        gens.append(_rwkv_pair(r[:, sl], kf[:, sl], v[:, sl], kkr[:, sl], a[:, sl], logd[:, sl], st_s[p],
                               rk[:, sl], lnw[:, sl], lnb[:, sl], consts))
    results = _interleave(gens)
    for p in range(n_pairs):
        sl = slice(p * LANES, (p + 1) * LANES)
        y, st_new = results[p]
        st_s[p] = st_new
        out_ref[0, :, sl] = ya[:, sl] + gate_b[:, sl] * (y * gate_g[:, sl])

    @pl.when(c == n_chunks - 1)
    def _fin():
        stout_ref[0] = st_s[...]


def _rwkv(rw, gates, ya, shift0, st0, p, t_valid):
    b, tp, cols = rw.shape
    width = ya.shape[2]
    n_pairs = width // LANES
    chunk = RWKV_CHUNK
    n_chunks = tp // chunk
    vec = lambda n: pl.BlockSpec((1, n), lambda i, j: (0, 0))
    lora = lambda n: pl.BlockSpec((n, width), lambda i, j: (0, 0))
    return pl.pallas_call(
        functools.partial(_rwkv_body, chunk=chunk, t_valid=t_valid, n_chunks=n_chunks),
        grid=(b, n_chunks),
        in_specs=[pl.BlockSpec((1, chunk, cols), lambda i, j: (i, j, 0)),
                  pl.BlockSpec((1, chunk, width), lambda i, j: (i, j, 1)),
                  pl.BlockSpec((1, chunk, width), lambda i, j: (i, j, 0)),
                  pl.BlockSpec((1, 8, cols), lambda i, j: (i, 0, 0)),
                  pl.BlockSpec((1, n_pairs, LANES, LANES), lambda i, j: (i, 0, 0, 0)),
                  vec(cols), vec(width), vec(width), lora(LANES), lora(LANES), lora(LANES),
                  vec(width), vec(width), vec(width), vec(width), vec(width)],
        out_specs=[pl.BlockSpec((1, chunk, width), lambda i, j: (i, j, 0)),
                   pl.BlockSpec((1, n_pairs, LANES, LANES), lambda i, j: (i, 0, 0, 0)),
                   pl.BlockSpec((1, 8, cols), lambda i, j: (i, 0, 0))],
        out_shape=[jax.ShapeDtypeStruct((b, tp, width), F32),
                   jax.ShapeDtypeStruct((b, n_pairs, LANES, LANES), F32),
                   jax.ShapeDtypeStruct((b, 8, cols), F32)],
        scratch_shapes=[pltpu.VMEM((n_pairs, LANES, LANES), F32), pltpu.VMEM((8, cols), F32)],
        compiler_params=_params(2),
        name="rwkv7",
    )(rw, gates, ya, shift0, st0, p["mu"], p["w0"], p["a0"], p["w2"], p["a2"], p["g2"],
      p["k_k"], p["k_a"], p["r_k"], p["lnx_w"], p["lnx_b"])


def _attn_body(x_ref, g_ref, wq_ref, mk_ref, mv_ref, wo_ref, o_ref):
    x = x_ref[0]
    q = jnp.dot(_rms(x, g_ref[...]).astype(BF16), wq_ref[...], preferred_element_type=F32)
    mk = mk_ref[0]
    mv = mv_ref[0]
    hd = x.shape[1] // CA_HEADS
    outs = []
    for h in range(CA_HEADS):
        sl = slice(h * hd, (h + 1) * hd)
        s = lax.dot_general(q[:, sl].astype(BF16), mk[:, sl], (((1,), (1,)), ((), ())),
                            preferred_element_type=F32) * (hd ** -0.5)
        e = jnp.exp(s - jnp.max(s, axis=-1, keepdims=True))
        pr = e / jnp.sum(e, axis=-1, keepdims=True)
        outs.append(jnp.dot(pr.astype(BF16), mv[:, sl], preferred_element_type=F32))
    o = jnp.concatenate(outs, axis=-1)
    o_ref[0] = x + jnp.dot(o.astype(BF16), wo_ref[...], preferred_element_type=F32)


def _cross_attn(x, g, wq, mk, mv, wo, tm):
    b, t, d = x.shape
    n_mem = mk.shape[1]
    return pl.pallas_call(
        _attn_body,
        grid=(b, t // tm),
        in_specs=[pl.BlockSpec((1, tm, d), lambda i, j: (i, j, 0)),
                  pl.BlockSpec((1, d), lambda i, j: (0, 0)),
                  pl.BlockSpec((d, d), lambda i, j: (0, 0)),
                  pl.BlockSpec((1, n_mem, d), lambda i, j: (i, 0, 0)),
                  pl.BlockSpec((1, n_mem, d), lambda i, j: (i, 0, 0)),
                  pl.BlockSpec((d, d), lambda i, j: (0, 0))],
        out_specs=pl.BlockSpec((1, tm, d), lambda i, j: (i, j, 0)),
        out_shape=jax.ShapeDtypeStruct((b, t, d), F32),
        compiler_params=_params(2),
        name="cross_attn",
    )(x, g, wq, mk, mv, wo)


def _router_body(x_ref, g_ref, wr_ref, br_ref, h_ref, gate_ref, idx_ref):
    h = _rms(x_ref[...], g_ref[...])
    h_ref[...] = h
    logits = _dot3(h, wr_ref[...]) + br_ref[...]
    lane = lax.broadcasted_iota(jnp.int32, logits.shape, 1)
    neg = jnp.float32(-jnp.inf)
    vals, idxs = [], []
    for _ in range(TOP_K):
        m = jnp.max(logits, axis=-1, keepdims=True)
        idx = jnp.min(jnp.where(logits == m, lane, LANES), axis=-1, keepdims=True)
        vals.append(m)
        idxs.append(idx)
        logits = jnp.where(lane == idx, neg, logits)
    es = [jnp.exp(vv - vals[0]) for vv in vals]
    den = es[0] + es[1] + es[2] + es[3]
    gate = jnp.zeros(logits.shape, F32)
    sel = jnp.zeros(logits.shape, jnp.int32)
    for j in range(TOP_K):
        gate = jnp.where(lane == j, es[j] / den, gate)
        sel = jnp.where(lane == j, idxs[j], sel)
    gate_ref[...] = gate
    idx_ref[...] = sel


def _router(x, g, wr_pad, br_pad, tm):
    m, d = x.shape
    return pl.pallas_call(
        _router_body,
        grid=(m // tm,),
        in_specs=[pl.BlockSpec((tm, d), lambda i: (i, 0)),
                  pl.BlockSpec((1, d), lambda i: (0, 0)),
                  pl.BlockSpec((d, LANES), lambda i: (0, 0)),
                  pl.BlockSpec((1, LANES), lambda i: (0, 0))],
        out_specs=[pl.BlockSpec((tm, d), lambda i: (i, 0)),
                   pl.BlockSpec((tm, LANES), lambda i: (i, 0)),
                   pl.BlockSpec((tm, LANES), lambda i: (i, 0))],
        out_shape=[jax.ShapeDtypeStruct((m, d), F32),
                   jax.ShapeDtypeStruct((m, LANES), F32),
                   jax.ShapeDtypeStruct((m, LANES), jnp.int32)],
        compiler_params=_params(),
        name="moe_router",
    )(x, g, wr_pad, br_pad)


def _wait_rows(make_copy, n_rows):
    for bit in reversed(range(MOE_BLOCK.bit_length())):
        size = 1 << bit

        @pl.when((n_rows & size) != 0)
        def _wait(size=size):
            make_copy(size).wait()


def _expert_body(blk_e_ref, n_valid_ref, n_used_ref, slot_ref, slot_next_ref, h_ref, wgu_ref, bgu_ref,
                 wdn_ref, bdn_ref, y_ref, xbuf, ybuf, gsem, ssem, *, n_blk):
    del blk_e_ref
    i = pl.program_id(0)
    n_used = n_used_ref[0]
    cur = lax.rem(i, 2)
    f = wdn_ref.shape[1]

    def for_rows(n_rows, row_fn):
        @pl.when(n_rows == MOE_BLOCK)
        def _full():
            def body(g, carry):
                for u in range(ROW_UNROLL):
                    row_fn(g * ROW_UNROLL + u)
                return carry
            lax.fori_loop(0, MOE_BLOCK // ROW_UNROLL, body, 0)

        @pl.when(n_rows != MOE_BLOCK)
        def _partial():
            def body(r, carry):
                row_fn(r)
                return carry
            lax.fori_loop(0, n_rows, body, 0)

    def gather_start(slots, n_rows, b):
        def row(r):
            tok = lax.shift_right_logical(slots[0, 0, r], TOP_K.bit_length() - 1)
            pltpu.make_async_copy(h_ref.at[pl.ds(tok, 1)], xbuf.at[b, pl.ds(r, 1)], gsem.at[b]).start()
        for_rows(n_rows, row)

    def gather_wait(n_rows, b):
        _wait_rows(lambda k: pltpu.make_async_copy(h_ref.at[pl.ds(0, k)], xbuf.at[b, pl.ds(0, k)], gsem.at[b]),
                   n_rows)

    def scatter_start(slots, n_rows, b):
        def row(r):
            pltpu.make_async_copy(ybuf.at[b, pl.ds(r, 1)], y_ref.at[pl.ds(slots[0, 0, r], 1)], ssem.at[b]).start()
        for_rows(n_rows, row)

    def scatter_wait(n_rows, b):
        _wait_rows(lambda k: pltpu.make_async_copy(ybuf.at[b, pl.ds(0, k)], y_ref.at[pl.ds(0, k)], ssem.at[b]),
                   n_rows)

    @pl.when(i == 0)
    def _first():
        xbuf[...] = jnp.zeros(xbuf.shape, F32)
        gather_start(slot_ref, n_valid_ref[0], 0)

    @pl.when(i + 1 < n_used)
    def _prefetch():
        gather_start(slot_next_ref, n_valid_ref[i + 1], 1 - cur)

    @pl.when(i < n_used)
    def _run():
        gather_wait(n_valid_ref[i], cur)

        @pl.when(i >= 2)
        def _free_ybuf():
            scatter_wait(n_valid_ref[i - 2], cur)

        hh = jnp.dot(xbuf[cur].astype(BF16), wgu_ref[0], preferred_element_type=F32) + bgu_ref[0]
        hg = jnp.minimum(hh[:, :f], SWIGLU_LIMIT)
        hl = jnp.clip(hh[:, f:], -SWIGLU_LIMIT, SWIGLU_LIMIT)
        act = hg * _sigmoid(SWIGLU_ALPHA * hg) * (hl + 1.0)
        ybuf[cur] = jnp.dot(act.astype(BF16), wdn_ref[0], preferred_element_type=F32) + bdn_ref[0]
        scatter_start(slot_ref, n_valid_ref[i], cur)

    @pl.when(i == n_blk - 1)
    def _drain():
        last = n_used - 1

        @pl.when(last >= 1)
        def _prev():
            scatter_wait(n_valid_ref[last - 1], lax.rem(last - 1, 2))

        @pl.when(last >= 0)
        def _last():
            scatter_wait(n_valid_ref[last], lax.rem(last, 2))


def _experts(h, slots, blk_e, n_valid, n_used, wgu, bgu, wdn, bdn):
    m, d = h.shape
    f = wdn.shape[1]
    n_blk = slots.shape[0]
    grid_spec = pltpu.PrefetchScalarGridSpec(
        num_scalar_prefetch=3,
        grid=(n_blk,),
        in_specs=[pl.BlockSpec((1, 1, MOE_BLOCK), lambda i, be, nv, nu: (i, 0, 0), memory_space=pltpu.SMEM),
                  pl.BlockSpec((1, 1, MOE_BLOCK), lambda i, be, nv, nu: (jnp.minimum(i + 1, n_blk - 1), 0, 0),
                               memory_space=pltpu.SMEM),
                  pl.BlockSpec(memory_space=pl.ANY),
                  pl.BlockSpec((1, d, 2 * f), lambda i, be, nv, nu: (be[i], 0, 0)),
                  pl.BlockSpec((1, 1, 2 * f), lambda i, be, nv, nu: (be[i], 0, 0)),
                  pl.BlockSpec((1, f, d), lambda i, be, nv, nu: (be[i], 0, 0)),
                  pl.BlockSpec((1, 1, d), lambda i, be, nv, nu: (be[i], 0, 0))],
        out_specs=pl.BlockSpec(memory_space=pl.ANY),
        scratch_shapes=[pltpu.VMEM((2, MOE_BLOCK, d), F32), pltpu.VMEM((2, MOE_BLOCK, d), F32),
                        pltpu.SemaphoreType.DMA((2,)), pltpu.SemaphoreType.DMA((2,))],
    )
    return pl.pallas_call(
        functools.partial(_expert_body, n_blk=n_blk),
        grid_spec=grid_spec,
        out_shape=jax.ShapeDtypeStruct((m * TOP_K, d), F32),
        compiler_params=_params(),
        name="moe_experts",
    )(blk_e, n_valid, n_used, slots, slots, h, wgu, bgu, wdn, bdn)


def _combine_body(gate_ref, x_ref, fg_ref, y4_ref, o_ref):
    d = x_ref.shape[1]
    acc = x_ref[...]
    gate = gate_ref[...]
    for k in range(TOP_K):
        acc = acc + gate[:, k:k + 1] * y4_ref[:, k * d:(k + 1) * d]
    o_ref[...] = _rms(acc, fg_ref[...])


def _combine(gate, x, fg, y4, tm):
    m, d = x.shape
    return pl.pallas_call(
        _combine_body,
        grid=(m // tm,),
        in_specs=[pl.BlockSpec((tm, LANES), lambda i: (i, 0)),
                  pl.BlockSpec((tm, d), lambda i: (i, 0)),
                  pl.BlockSpec((1, d), lambda i: (0, 0)),
                  pl.BlockSpec((tm, TOP_K * d), lambda i: (i, 0))],
        out_specs=pl.BlockSpec((tm, d), lambda i: (i, 0)),
        out_shape=jax.ShapeDtypeStruct((m, d), F32),
        compiler_params=_params(),
        name="moe_combine",
    )(gate, x, fg, y4)


def _moe_final(x2d, w, tm):
    m, d = x2d.shape
    h, gate, sel = _router(x2d, w["norm_ffn_g"], w["router_w"], w["router_b"], tm)
    flat_e = sel[:, :TOP_K].reshape(-1)
    n_assign = m * TOP_K
    order = jnp.argsort(flat_e, stable=True).astype(jnp.int32)
    experts = jnp.arange(N_EXPERTS, dtype=jnp.int32)
    counts = jnp.sum((flat_e[:, None] == experts[None, :]).astype(jnp.int32), axis=0)
    padded = (counts + MOE_BLOCK - 1) // MOE_BLOCK * MOE_BLOCK
    pad_end = jnp.cumsum(padded)
    pad_start = pad_end - padded
    start = jnp.cumsum(counts) - counts
    n_blk = -(-(n_assign + N_EXPERTS * (MOE_BLOCK - 1)) // MOE_BLOCK)
    blk_row0 = jnp.arange(n_blk, dtype=jnp.int32) * MOE_BLOCK
    blk_e = jnp.minimum(jnp.searchsorted(pad_end, blk_row0, side="right"), N_EXPERTS - 1).astype(jnp.int32)
    n_used = (pad_end[-1:] // MOE_BLOCK).astype(jnp.int32)
    off = blk_row0 - pad_start[blk_e]
    n_valid = jnp.clip(counts[blk_e] - off, 0, MOE_BLOCK).astype(jnp.int32)
    within = jnp.arange(MOE_BLOCK, dtype=jnp.int32)[None, :]
    pos = jnp.clip((start[blk_e] + off)[:, None] + within, 0, n_assign - 1)
    slots = jnp.where(within < n_valid[:, None], order[pos], 0).astype(jnp.int32).reshape(n_blk, 1, MOE_BLOCK)
    y = _experts(h, slots, blk_e, n_valid, n_used, w["moe_w_gu"], w["moe_b_gu"], w["moe_w_down"], w["moe_b_down"])
    return _combine(gate, x2d, w["final_norm_g"], y.reshape(m, TOP_K * d), tm)


def _layer(x, shift_prev, s0, mk, mv, w):
    b, t, d = x.shape
    m = b * t
    tm = min(256, m)
    x2d = x.reshape(m, d)
    uv = _norm_matmul(x2d, w["norm_mix_g"], w["w_uv"], tm)
    rw = _norm_matmul(x2d, w["norm_mix_g"], w["w_rw"], tm)
    gates = _norm_matmul(x2d, w["norm_mix_g"], w["w_gates"], tm)
    cols = rw.shape[1]

    rows = min(t, GMLP_CHUNK)
    ws = w["gmlp_ws"][:, :rows, :rows]
    bs_full = jnp.repeat(w["gmlp_bs"][:, :rows].T, d // GMLP_GROUPS, axis=1)
    ya, v_n = _gmlp(uv, gates, w["gmlp_v_norm_g"], ws, bs_full, rows)

    n_heads = s0.shape[1]
    n_pairs = n_heads // 2
    tp = -(-t // RWKV_CHUNK) * RWKV_CHUNK
    pad3 = lambda z: jnp.pad(z.reshape(b, t, -1), ((0, 0), (0, tp - t), (0, 0)))
    st_t = jnp.swapaxes(s0, 2, 3).reshape(b, n_pairs, 2, RWKV_HEAD_DIM, RWKV_HEAD_DIM)
    st0 = jnp.zeros((b, n_pairs, LANES, LANES), F32)
    st0 = st0.at[:, :, :RWKV_HEAD_DIM, :RWKV_HEAD_DIM].set(st_t[:, :, 0])
    st0 = st0.at[:, :, RWKV_HEAD_DIM:, RWKV_HEAD_DIM:].set(st_t[:, :, 1])
    shift0 = jnp.broadcast_to(shift_prev, (b, 8, cols))
    merged, st_out, last = _rwkv(pad3(rw), pad3(gates), pad3(ya), shift0, st0, w, t)
    merged = merged[:, :t].reshape(m, d)
    s_new = jnp.stack([st_out[:, :, :RWKV_HEAD_DIM, :RWKV_HEAD_DIM], st_out[:, :, RWKV_HEAD_DIM:, RWKV_HEAD_DIM:]],
                      axis=2).reshape(b, n_heads, RWKV_HEAD_DIM, RWKV_HEAD_DIM)
    s_new = jnp.swapaxes(s_new, 2, 3)
    shift_new = last[:, :1, :]

    x1 = _matmul_res(merged, w["w_out"], x2d, tm)
    x2 = _cross_attn(x1.reshape(b, t, d), w["norm_ca_g"], w["ca_wq"], mk, mv, w["ca_wo"], min(512, t))
    y = _moe_final(x2.reshape(m, d), w, min(256, m))
    return y.reshape(b, t, d), shift_new, s_new, v_n.reshape(b, t, d)


def kernel(x_prompt, x_sample, mem_prompt, state_shift, state_rwkv, cache_mem_k, cache_mem_v, norm_mix_g, w_in, gmlp_v_norm_g, gmlp_ws, gmlp_bs, rwkv_mu, rwkv_w0, rwkv_w2, rwkv_a0, rwkv_a2, rwkv_g2, rwkv_k_k, rwkv_k_a, rwkv_r_k, rwkv_lnx_w, rwkv_lnx_b, w_out, norm_ca_g, norm_mem_g, ca_wq, ca_wk, ca_wv, ca_wo, norm_ffn_g, router_w, router_b, moe_w_gu, moe_b_gu, moe_w_down, moe_b_down, final_norm_g):
    depth = w_in.shape[0]
    assert depth == 1, "the fused MoE + final-norm tail assumes a single layer"
    l = 0
    bp, _, d = x_prompt.shape
    width = rwkv_w0.shape[1]
    n_heads = width // RWKV_HEAD_DIM
    gw = gmlp_v_norm_g.shape[1]
    row = lambda z: z.reshape(1, -1)
    zpad = jnp.zeros((LANES - rwkv_w2.shape[1], width), F32)
    w = dict(
        norm_mix_g=row(norm_mix_g[l]),
        w_uv=w_in[l][:, :2 * gw].astype(BF16),
        w_rw=w_in[l][:, 2 * gw:w_in.shape[2] - 2 * d].astype(BF16),
        w_gates=w_in[l][:, w_in.shape[2] - 2 * d:].astype(BF16),
        gmlp_v_norm_g=row(gmlp_v_norm_g[l]), gmlp_ws=gmlp_ws[l], gmlp_bs=gmlp_bs[l],
        mu=row(rwkv_mu[l]), w0=row(rwkv_w0[l]), a0=row(rwkv_a0[l]),
        w2=jnp.concatenate([rwkv_w2[l], zpad], axis=0).astype(BF16),
        a2=jnp.concatenate([zpad, rwkv_a2[l]], axis=0).astype(BF16),
        g2=rwkv_g2[l].astype(BF16),
        k_k=row(rwkv_k_k[l]), k_a=row(rwkv_k_a[l]), r_k=row(rwkv_r_k[l]),
        lnx_w=row(rwkv_lnx_w[l]), lnx_b=row(rwkv_lnx_b[l]),
        w_out=w_out[l].astype(BF16), norm_ca_g=row(norm_ca_g[l]),
        ca_wq=ca_wq[l].astype(BF16), ca_wo=ca_wo[l].astype(BF16),
        norm_ffn_g=row(norm_ffn_g[l]),
        router_w=jnp.pad(router_w[l], ((0, 0), (0, LANES - N_EXPERTS))),
        router_b=jnp.pad(row(router_b[l]), ((0, 0), (0, LANES - N_EXPERTS)), constant_values=-jnp.inf),
        moe_w_gu=moe_w_gu[l].astype(BF16), moe_b_gu=moe_b_gu[l][:, None, :],
        moe_w_down=moe_w_down[l].astype(BF16), moe_b_down=moe_b_down[l][:, None, :],
        final_norm_g=row(final_norm_g),
    )

    n_mem = mem_prompt.shape[1]
    mem2d = mem_prompt.reshape(bp * n_mem, d)
    mk = _norm_matmul(mem2d, row(norm_mem_g[l]), ca_wk[l].astype(BF16), 256).reshape(bp, n_mem, d)
    mv = _norm_matmul(mem2d, row(norm_mem_g[l]), ca_wv[l].astype(BF16), 256).reshape(bp, n_mem, d)
    shift0 = jnp.zeros((bp, 1, rwkv_mu.shape[1]), F32)
    s_zero = jnp.zeros((bp, n_heads, RWKV_HEAD_DIM, RWKV_HEAD_DIM), F32)
    y_p, sh_p, st_p, _ = _layer(x_prompt, shift0, s_zero, mk.astype(BF16), mv.astype(BF16), w)

    bs = x_sample.shape[0]
    cmk = cache_mem_k[l].reshape(bs, n_mem, d).astype(BF16)
    cmv = cache_mem_v[l].reshape(bs, n_mem, d).astype(BF16)
    y_s, sh_s, st_s, v_s = _layer(x_sample, state_shift[l], state_rwkv[l], cmk, cmv, w)

    ca_shape = (1, bp, n_mem, CA_HEADS, d // CA_HEADS)
    return (y_p, y_s, sh_p[None], st_p[None], mk.reshape(ca_shape), mv.reshape(ca_shape),
            sh_s[None], st_s[None], v_s[None])
```

```python
import functools

import jax
import jax.numpy as jnp
from jax import lax
from jax.experimental import pallas as pl
from jax.experimental.pallas import tpu as pltpu

F32 = jnp.float32
BF16 = jnp.bfloat16

NORM_EPS = 1e-5
GN_EPS = 64e-5
RWKV_HEAD_DIM = 64
RWKV_CHUNK = 64
GMLP_CHUNK = 128
GMLP_GROUPS = 8
CA_HEADS = 4
N_EXPERTS = 32
TOP_K = 4
SWIGLU_LIMIT = 7.0
SWIGLU_ALPHA = 1.702
LANES = 128
MOE_BLOCK = 256
ROW_UNROLL = 8
VMEM_LIMIT = 56 * 1024 * 1024


def _params(n_axes=1):
    return pltpu.CompilerParams(dimension_semantics=("arbitrary",) * n_axes, vmem_limit_bytes=VMEM_LIMIT)


def _rms(x, g):
    return x * lax.rsqrt(jnp.mean(x * x, axis=-1, keepdims=True) + NORM_EPS) * g


def _gelu(x):
    return x * (lax.erf(x * 0.7071067811865476) + 1.0) * 0.5


def _sigmoid(x):
    return 1.0 / (1.0 + jnp.exp(-x))


def _bdot(a, b):
    return jnp.dot(a.astype(BF16), b.astype(BF16), preferred_element_type=F32)


def _hi_lo(x):
    hi = x.astype(BF16)
    lo = (x - hi.astype(F32)).astype(BF16)
    return hi, lo


def _dot3(a, b, mode="nn"):
    ah, al = _hi_lo(a)
    bh, bl = _hi_lo(b)
    if mode == "nn":
        lhs = jnp.concatenate([ah, ah, al], axis=1)
        rhs = jnp.concatenate([bh, bl, bh], axis=0)
        dims = (((1,), (0,)), ((), ()))
    elif mode == "nt":
        lhs = jnp.concatenate([ah, ah, al], axis=1)
        rhs = jnp.concatenate([bh, bl, bh], axis=1)
        dims = (((1,), (1,)), ((), ()))
    else:
        lhs = jnp.concatenate([ah, ah, al], axis=0)
        rhs = jnp.concatenate([bh, bl, bh], axis=0)
        dims = (((0,), (0,)), ((), ()))
    return lax.dot_general(lhs, rhs, dims, preferred_element_type=F32)


def _dot2_exact_rhs(a, b_bf16):
    ah, al = _hi_lo(a)
    lhs = jnp.concatenate([ah, al], axis=1)
    rhs = jnp.concatenate([b_bf16, b_bf16], axis=0)
    return jnp.dot(lhs, rhs, preferred_element_type=F32)


def _dot2_exact_lhs(a_bf16, b):
    bh, bl = _hi_lo(b)
    lhs = jnp.concatenate([a_bf16, a_bf16], axis=1)
    rhs = jnp.concatenate([bh, bl], axis=0)
    return jnp.dot(lhs, rhs, preferred_element_type=F32)


def _norm_matmul_body(x_ref, g_ref, w_ref, o_ref):
    xn = _rms(x_ref[...], g_ref[...])
    o_ref[...] = jnp.dot(xn.astype(BF16), w_ref[...], preferred_element_type=F32)


def _norm_matmul(x, g, w, tm):
    m, k = x.shape
    n = w.shape[1]
    return pl.pallas_call(
        _norm_matmul_body,
        grid=(m // tm,),
        in_specs=[pl.BlockSpec((tm, k), lambda i: (i, 0)),
                  pl.BlockSpec((1, k), lambda i: (0, 0)),
                  pl.BlockSpec((k, n), lambda i: (0, 0))],
        out_specs=pl.BlockSpec((tm, n), lambda i: (i, 0)),
        out_shape=jax.ShapeDtypeStruct((m, n), F32),
        compiler_params=_params(),
        name="norm_matmul",
    )(x, g, w)


def _sum_matmul_res_body(a_ref, w_ref, r_ref, o_ref):
    o_ref[...] = r_ref[...] + jnp.dot(a_ref[...].astype(BF16), w_ref[...], preferred_element_type=F32)


def _matmul_res(a, w, res, tm):
    m, k = a.shape
    n = w.shape[1]
    return pl.pallas_call(
        _sum_matmul_res_body,
        grid=(m // tm,),
        in_specs=[pl.BlockSpec((tm, k), lambda i: (i, 0)),
                  pl.BlockSpec((k, n), lambda i: (0, 0)),
                  pl.BlockSpec((tm, n), lambda i: (i, 0))],
        out_specs=pl.BlockSpec((tm, n), lambda i: (i, 0)),
        out_shape=jax.ShapeDtypeStruct((m, n), F32),
        compiler_params=_params(),
        name="matmul_res",
    )(a, w, res)


def _gmlp_body(u_ref, v_ref, ga_ref, vg_ref, ws_ref, bs_ref, ya_ref, vn_ref, *, rows):
    gv = _gelu(v_ref[...])
    vn = _rms(gv, vg_ref[...])
    vn_ref[...] = vn
    ri = lax.broadcasted_iota(jnp.int32, (rows, rows), 0)
    ci = lax.broadcasted_iota(jnp.int32, (rows, rows), 1)
    tril = ri >= ci
    gu = _gelu(u_ref[...])
    gate = _sigmoid(ga_ref[...])
    bias = bs_ref[...]
    width = vn.shape[1] // GMLP_GROUPS
    for g in range(GMLP_GROUPS):
        sl = slice(g * width, (g + 1) * width)
        w = jnp.where(tril, ws_ref[g], 0.0)
        sp = _bdot(w, vn[:, sl]) + bias[:, sl]
        ya_ref[:, sl] = gate[:, sl] * (gu[:, sl] * sp)


def _gmlp(uv, gates, vg, ws, bs_full, rows):
    m = uv.shape[0]
    w = uv.shape[1] // 2
    return pl.pallas_call(
        functools.partial(_gmlp_body, rows=rows),
        grid=(m // rows,),
        in_specs=[pl.BlockSpec((rows, w), lambda i: (i, 0)),
                  pl.BlockSpec((rows, w), lambda i: (i, 1)),
                  pl.BlockSpec((rows, w), lambda i: (i, 0)),
                  pl.BlockSpec((1, w), lambda i: (0, 0)),
                  pl.BlockSpec((GMLP_GROUPS, rows, rows), lambda i: (0, 0, 0)),
                  pl.BlockSpec((rows, w), lambda i: (0, 0))],
        out_specs=[pl.BlockSpec((rows, w), lambda i: (i, 0)),
                   pl.BlockSpec((rows, w), lambda i: (i, 0))],
        out_shape=[jax.ShapeDtypeStruct((m, w), F32), jax.ShapeDtypeStruct((m, w), F32)],
        compiler_params=_params(),
        name="gmlp",
    )(uv, uv, gates, vg, ws, bs_full)


def _rwkv_pair(r, kf, v, kkr, a, logd, st2, rk, lnw, lnb, consts):
    m0, strict, incl, bd, eye, bd_ones, tril_ones, n_levels = consts
    chunk = r.shape[0]

    def stack(z):
        return jnp.concatenate([jnp.where(m0, z, 0.0), jnp.where(m0, 0.0, z)], axis=0)

    ss = _dot2_exact_rhs(kkr * kkr, bd_ones)
    cl = _dot2_exact_lhs(tril_ones, logd)
    bonus = _dot2_exact_rhs(r * kf * rk, bd_ones) * v
    yield
    kk = kkr / jnp.maximum(jnp.sqrt(ss), 1e-12)
    cl_end = cl[chunk - 1:chunk, :]
    d_in = jnp.exp(cl)
    d_inv = jnp.exp(-cl)
    a_t = -(kk * jnp.exp(cl - logd))
    kb = kk * a
    b_t = kb * d_inv
    k_t = kf * d_inv
    r_t = r * d_in
    d_tail = jnp.exp(cl_end - cl)
    d_end = jnp.exp(cl_end)

    g = _dot3(jnp.concatenate([a_t, r_t], axis=0),
              jnp.concatenate([stack(b_t), stack(k_t)], axis=0), mode="nt")
    yield
    n_cat = jnp.where(strict, g[:chunk, :LANES], 0.0)
    a_ak = jnp.where(strict, g[:chunk, LANES:], 0.0)
    p_cat = jnp.where(incl, g[chunk:, :LANES], 0.0)
    q_cat = jnp.where(incl, g[chunk:, LANES:], 0.0)

    v_st = stack(v)
    x = _dot3(jnp.concatenate([a_t, a_ak], axis=1), jnp.concatenate([st2, v_st], axis=0))
    yield
    n_pow = n_cat
    for lvl in range(n_levels):
        if lvl < n_levels - 1:
            z = _dot3(n_pow, jnp.concatenate([stack(x), stack(n_pow)], axis=1))
            yield
            x = x + z[:, :LANES]
            n_pow = z[:, LANES:]
        else:
            z = _dot3(n_pow, stack(x))
            yield
            x = x + z
    u = x

    o = _dot3(jnp.concatenate([r_t, p_cat, q_cat], axis=1),
              jnp.concatenate([st2, stack(u), v_st], axis=0))
    e_diag = jnp.where(eye, jnp.broadcast_to(d_end, (LANES, LANES)), 0.0)
    st_new = _dot3(jnp.concatenate([e_diag, kb * d_tail, kf * d_tail], axis=0),
                   jnp.concatenate([st2, u, v], axis=0), mode="tn")
    yield
    st_new = jnp.where(bd, st_new, 0.0)

    inv_n = 1.0 / RWKV_HEAD_DIM
    mu = _dot2_exact_rhs(o, bd_ones) * inv_n
    yield
    dlt = o - mu
    var = _dot2_exact_rhs(dlt * dlt, bd_ones) * inv_n
    yield
    on = dlt * lax.rsqrt(var + GN_EPS) * lnw + lnb
    return on + bonus, st_new


def _interleave(generators):
    results = [None] * len(generators)
    live = list(range(len(generators)))
    while live:
        still = []
        for i in live:
            try:
                next(generators[i])
                still.append(i)
            except StopIteration as stop:
                results[i] = stop.value
        live = still
    return results


def _rwkv_body(rw_ref, gb_ref, ya_ref, sh0_ref, st0_ref, mu_ref, w0_ref, a0_ref, w2_ref, a2_ref, g2_ref,
               kk_ref, ka_ref, rk_ref, lnw_ref, lnb_ref, out_ref, stout_ref, last_ref, st_s, carry_s,
               *, chunk, t_valid, n_chunks):
    c = pl.program_id(1)
    width = out_ref.shape[2]
    n_pairs = width // LANES

    @pl.when(c == 0)
    def _init():
        st_s[...] = st0_ref[0]
        carry_s[...] = sh0_ref[0]

    rw = rw_ref[0]
    cols = rw.shape[1]
    row_w = lax.broadcasted_iota(jnp.int32, (chunk, cols), 0)
    rw_prev = jnp.where(row_w == 0, carry_s[0:1, :], pltpu.roll(rw, 1, axis=0))
    last_row = rw[(t_valid - 1) % chunk:(t_valid - 1) % chunk + 1, :]
    carry_s[0:1, :] = last_row
    last_ref[0] = jnp.broadcast_to(last_row, (8, cols))
    mix = rw + (rw_prev - rw) * mu_ref[...]

    r = mix[:, 0:width]
    k = mix[:, width:2 * width]
    v = mix[:, 2 * width:3 * width]
    wa = mix[:, 3 * width:3 * width + LANES]
    gl = mix[:, 3 * width + LANES:3 * width + 2 * LANES]
    z = -(w0_ref[...] + _bdot(jnp.tanh(wa), w2_ref[...]))
    w = -(jnp.maximum(z, 0.0) + jnp.log1p(jnp.exp(-jnp.abs(z)))) - 0.5
    logd = -jnp.exp(w)
    a = _sigmoid(a0_ref[...] + _bdot(wa, a2_ref[...]))
    gate_g = _bdot(_sigmoid(gl), g2_ref[...])
    kkr = k * kk_ref[...]
    kf = k * (1.0 + (a - 1.0) * ka_ref[...])
    if t_valid < chunk * n_chunks:
        valid = lax.broadcasted_iota(jnp.int32, (chunk, width), 0) + c * chunk < t_valid
        logd = jnp.where(valid, logd, 0.0)
        kkr = jnp.where(valid, kkr, 0.0)
        kf = jnp.where(valid, kf, 0.0)
        v = jnp.where(valid, v, 0.0)

    lane = lax.broadcasted_iota(jnp.int32, (chunk, LANES), 1)
    rowi = lax.broadcasted_iota(jnp.int32, (chunk, LANES), 0)
    m0 = lane < RWKV_HEAD_DIM
    lane_in_head = jnp.where(m0, lane, lane - RWKV_HEAD_DIM)
    r2 = lax.broadcasted_iota(jnp.int32, (LANES, LANES), 0)
    c2 = lax.broadcasted_iota(jnp.int32, (LANES, LANES), 1)
    bd = (r2 < RWKV_HEAD_DIM) == (c2 < RWKV_HEAD_DIM)
    rl = lax.broadcasted_iota(jnp.int32, (chunk, chunk), 0)
    cl_ = lax.broadcasted_iota(jnp.int32, (chunk, chunk), 1)
    consts = (m0, rowi > lane_in_head, rowi >= lane_in_head, bd, r2 == c2,
              jnp.where(bd, 1.0, 0.0).astype(BF16), jnp.where(rl >= cl_, 1.0, 0.0).astype(BF16),
              chunk.bit_length() - 1)

    gate_b = _sigmoid(gb_ref[0])
    ya = ya_ref[0]
    rk = rk_ref[...]
    lnw = lnw_ref[...]
    lnb = lnb_ref[...]
    gens = []
    for p in range(n_pairs):
        sl = slice(p * LANES, (p + 1) * LANES)
        gens.append(_rwkv_pair(r[:, sl], kf[:, sl], v[:, sl], kkr[:, sl], a[:, sl], logd[:, sl], st_s[p],
                               rk[:, sl], lnw[:, sl], lnb[:, sl], consts))
    results = _interleave(gens)
    for p in range(n_pairs):
        sl = slice(p * LANES, (p + 1) * LANES)
        y, st_new = results[p]
        st_s[p] = st_new
        out_ref[0, :, sl] = ya[:, sl] + gate_b[:, sl] * (y * gate_g[:, sl])

    @pl.when(c == n_chunks - 1)
    def _fin():
        stout_ref[0] = st_s[...]


def _rwkv(rw, gates, ya, shift0, st0, p, t_valid):
    b, tp, cols = rw.shape
    width = ya.shape[2]
    n_pairs = width // LANES
    chunk = RWKV_CHUNK
    n_chunks = tp // chunk
    vec = lambda n: pl.BlockSpec((1, n), lambda i, j: (0, 0))
    lora = lambda n: pl.BlockSpec((n, width), lambda i, j: (0, 0))
    return pl.pallas_call(
        functools.partial(_rwkv_body, chunk=chunk, t_valid=t_valid, n_chunks=n_chunks),
        grid=(b, n_chunks),
        in_specs=[pl.BlockSpec((1, chunk, cols), lambda i, j: (i, j, 0)),
                  pl.BlockSpec((1, chunk, width), lambda i, j: (i, j, 1)),
                  pl.BlockSpec((1, chunk, width), lambda i, j: (i, j, 0)),
                  pl.BlockSpec((1, 8, cols), lambda i, j: (i, 0, 0)),
                  pl.BlockSpec((1, n_pairs, LANES, LANES), lambda i, j: (i, 0, 0, 0)),
                  vec(cols), vec(width), vec(width), lora(LANES), lora(LANES), lora(LANES),
                  vec(width), vec(width), vec(width), vec(width), vec(width)],
        out_specs=[pl.BlockSpec((1, chunk, width), lambda i, j: (i, j, 0)),
                   pl.BlockSpec((1, n_pairs, LANES, LANES), lambda i, j: (i, 0, 0, 0)),
                   pl.BlockSpec((1, 8, cols), lambda i, j: (i, 0, 0))],
        out_shape=[jax.ShapeDtypeStruct((b, tp, width), F32),
                   jax.ShapeDtypeStruct((b, n_pairs, LANES, LANES), F32),
                   jax.ShapeDtypeStruct((b, 8, cols), F32)],
        scratch_shapes=[pltpu.VMEM((n_pairs, LANES, LANES), F32), pltpu.VMEM((8, cols), F32)],
        compiler_params=_params(2),
        name="rwkv7",
    )(rw, gates, ya, shift0, st0, p["mu"], p["w0"], p["a0"], p["w2"], p["a2"], p["g2"],
      p["k_k"], p["k_a"], p["r_k"], p["lnx_w"], p["lnx_b"])


def _attn_body(x_ref, g_ref, wq_ref, mk_ref, mv_ref, wo_ref, o_ref):
    x = x_ref[0]
    q = jnp.dot(_rms(x, g_ref[...]).astype(BF16), wq_ref[...], preferred_element_type=F32)
    mk = mk_ref[0]
    mv = mv_ref[0]
    hd = x.shape[1] // CA_HEADS
    outs = []
    for h in range(CA_HEADS):
        sl = slice(h * hd, (h + 1) * hd)
        s = lax.dot_general(q[:, sl].astype(BF16), mk[:, sl], (((1,), (1,)), ((), ())),
                            preferred_element_type=F32) * (hd ** -0.5)
        e = jnp.exp(s - jnp.max(s, axis=-1, keepdims=True))
        pr = e / jnp.sum(e, axis=-1, keepdims=True)
        outs.append(jnp.dot(pr.astype(BF16), mv[:, sl], preferred_element_type=F32))
    o = jnp.concatenate(outs, axis=-1)
    o_ref[0] = x + jnp.dot(o.astype(BF16), wo_ref[...], preferred_element_type=F32)


def _cross_attn(x, g, wq, mk, mv, wo, tm):
    b, t, d = x.shape
    n_mem = mk.shape[1]
    return pl.pallas_call(
        _attn_body,
        grid=(b, t // tm),
        in_specs=[pl.BlockSpec((1, tm, d), lambda i, j: (i, j, 0)),
                  pl.BlockSpec((1, d), lambda i, j: (0, 0)),
                  pl.BlockSpec((d, d), lambda i, j: (0, 0)),
                  pl.BlockSpec((1, n_mem, d), lambda i, j: (i, 0, 0)),
                  pl.BlockSpec((1, n_mem, d), lambda i, j: (i, 0, 0)),
                  pl.BlockSpec((d, d), lambda i, j: (0, 0))],
        out_specs=pl.BlockSpec((1, tm, d), lambda i, j: (i, j, 0)),
        out_shape=jax.ShapeDtypeStruct((b, t, d), F32),
        compiler_params=_params(2),
        name="cross_attn",
    )(x, g, wq, mk, mv, wo)


def _store_token_tiles(ref, rows):
    n, d = rows.shape
    sub = d // LANES
    for s in range(sub):
        ref[pl.ds(s, n, stride=sub), :] = rows[:, s * LANES:(s + 1) * LANES]


def _load_token_tiles(ref, sub):
    n = ref.shape[0] // sub
    return jnp.concatenate([ref[pl.ds(s, n, stride=sub), :] for s in range(sub)], axis=1)


def _router_body(x_ref, g_ref, wr_ref, br_ref, h_ref, gate_ref, idx_ref):
    h = _rms(x_ref[...], g_ref[...])
    _store_token_tiles(h_ref, h)
    logits = _dot3(h, wr_ref[...]) + br_ref[...]
    lane = lax.broadcasted_iota(jnp.int32, logits.shape, 1)
    neg = jnp.float32(-jnp.inf)
    vals, idxs = [], []
    for _ in range(TOP_K):
        m = jnp.max(logits, axis=-1, keepdims=True)
        idx = jnp.min(jnp.where(logits == m, lane, LANES), axis=-1, keepdims=True)
        vals.append(m)
        idxs.append(idx)
        logits = jnp.where(lane == idx, neg, logits)
    es = [jnp.exp(vv - vals[0]) for vv in vals]
    den = es[0] + es[1] + es[2] + es[3]
    gate = jnp.zeros(logits.shape, F32)
    sel = jnp.zeros(logits.shape, jnp.int32)
    for j in range(TOP_K):
        gate = jnp.where(lane == j, es[j] / den, gate)
        sel = jnp.where(lane == j, idxs[j], sel)
    gate_ref[...] = gate
    idx_ref[...] = sel


def _router(x, g, wr_pad, br_pad, tm):
    m, d = x.shape
    return pl.pallas_call(
        _router_body,
        grid=(m // tm,),
        in_specs=[pl.BlockSpec((tm, d), lambda i: (i, 0)),
                  pl.BlockSpec((1, d), lambda i: (0, 0)),
                  pl.BlockSpec((d, LANES), lambda i: (0, 0)),
                  pl.BlockSpec((1, LANES), lambda i: (0, 0))],
        out_specs=[pl.BlockSpec((tm * d // LANES, LANES), lambda i: (i, 0)),
                   pl.BlockSpec((tm, LANES), lambda i: (i, 0)),
                   pl.BlockSpec((tm, LANES), lambda i: (i, 0))],
        out_shape=[jax.ShapeDtypeStruct((m * d // LANES, LANES), F32),
                   jax.ShapeDtypeStruct((m, LANES), F32),
                   jax.ShapeDtypeStruct((m, LANES), jnp.int32)],
        compiler_params=_params(),
        name="moe_router",
    )(x, g, wr_pad, br_pad)


def _wait_rows(make_copy, n_rows):
    for bit in reversed(range(MOE_BLOCK.bit_length())):
        size = 1 << bit

        @pl.when((n_rows & size) != 0)
        def _wait(size=size):
            make_copy(size).wait()


def _expert_body(blk_e_ref, n_valid_ref, n_used_ref, slot_ref, slot_next_ref, h_ref, wgu_ref, bgu_ref,
                 wdn_ref, bdn_ref, y_ref, xbuf, ybuf, gsem, ssem, *, n_blk):
    del blk_e_ref
    i = pl.program_id(0)
    n_used = n_used_ref[0]
    cur = lax.rem(i, 2)
    f = wdn_ref.shape[1]
    sub = wdn_ref.shape[2] // LANES
    m = h_ref.shape[0] // sub

    def token_of(slot):
        return (slot & (m - 1)) if m & (m - 1) == 0 else lax.rem(slot, m)

    def tiles(first, count=1):
        start = first * sub if isinstance(first, int) else pl.multiple_of(first * sub, sub)
        return pl.ds(start, count * sub)

    def for_rows(n_rows, row_fn):
        @pl.when(n_rows == MOE_BLOCK)
        def _full():
            def body(g, carry):
                for u in range(ROW_UNROLL):
                    row_fn(g * ROW_UNROLL + u)
                return carry
            lax.fori_loop(0, MOE_BLOCK // ROW_UNROLL, body, 0)

        @pl.when(n_rows != MOE_BLOCK)
        def _partial():
            def body(r, carry):
                row_fn(r)
                return carry
            lax.fori_loop(0, n_rows, body, 0)

    def gather_start(slots, n_rows, b):
        def row(r):
            pltpu.make_async_copy(h_ref.at[tiles(token_of(slots[0, 0, r]))], xbuf.at[b, tiles(r)],
                                  gsem.at[b]).start()
        for_rows(n_rows, row)

    def gather_wait(n_rows, b):
        _wait_rows(lambda k: pltpu.make_async_copy(h_ref.at[tiles(0, k)], xbuf.at[b, tiles(0, k)], gsem.at[b]),
                   n_rows)

    def scatter_start(slots, n_rows, b):
        def row(r):
            pltpu.make_async_copy(ybuf.at[b, tiles(r)], y_ref.at[tiles(slots[0, 0, r])], ssem.at[b]).start()
        for_rows(n_rows, row)

    def scatter_wait(n_rows, b):
        _wait_rows(lambda k: pltpu.make_async_copy(ybuf.at[b, tiles(0, k)], y_ref.at[tiles(0, k)], ssem.at[b]),
                   n_rows)

    @pl.when(i == 0)
    def _first():
        xbuf[...] = jnp.zeros(xbuf.shape, F32)
        gather_start(slot_ref, n_valid_ref[0], 0)

    @pl.when(i + 1 < n_used)
    def _prefetch():
        gather_start(slot_next_ref, n_valid_ref[i + 1], 1 - cur)

    @pl.when(i < n_used)
    def _run():
        gather_wait(n_valid_ref[i], cur)

        @pl.when(i >= 2)
        def _free_ybuf():
            scatter_wait(n_valid_ref[i - 2], cur)

        x = _load_token_tiles(xbuf.at[cur], sub)
        hh = jnp.dot(x.astype(BF16), wgu_ref[0], preferred_element_type=F32) + bgu_ref[0]
        hg = jnp.minimum(hh[:, :f], SWIGLU_LIMIT)
        hl = jnp.clip(hh[:, f:], -SWIGLU_LIMIT, SWIGLU_LIMIT)
        act = hg * _sigmoid(SWIGLU_ALPHA * hg) * (hl + 1.0)
        _store_token_tiles(ybuf.at[cur], jnp.dot(act.astype(BF16), wdn_ref[0], preferred_element_type=F32)
                           + bdn_ref[0])
        scatter_start(slot_ref, n_valid_ref[i], cur)

    @pl.when(i == n_blk - 1)
    def _drain():
        last = n_used - 1

        @pl.when(last >= 1)
        def _prev():
            scatter_wait(n_valid_ref[last - 1], lax.rem(last - 1, 2))

        @pl.when(last >= 0)
        def _last():
            scatter_wait(n_valid_ref[last], lax.rem(last, 2))


def _experts(h, slots, blk_e, n_valid, n_used, wgu, bgu, wdn, bdn):
    f, d = wdn.shape[1:]
    n_sub = d // LANES
    n_blk = slots.shape[0]
    grid_spec = pltpu.PrefetchScalarGridSpec(
        num_scalar_prefetch=3,
        grid=(n_blk,),
        in_specs=[pl.BlockSpec((1, 1, MOE_BLOCK), lambda i, be, nv, nu: (i, 0, 0), memory_space=pltpu.SMEM),
                  pl.BlockSpec((1, 1, MOE_BLOCK), lambda i, be, nv, nu: (jnp.minimum(i + 1, n_blk - 1), 0, 0),
                               memory_space=pltpu.SMEM),
                  pl.BlockSpec(memory_space=pl.ANY),
                  pl.BlockSpec((1, d, 2 * f), lambda i, be, nv, nu: (be[i], 0, 0)),
                  pl.BlockSpec((1, 1, 2 * f), lambda i, be, nv, nu: (be[i], 0, 0)),
                  pl.BlockSpec((1, f, d), lambda i, be, nv, nu: (be[i], 0, 0)),
                  pl.BlockSpec((1, 1, d), lambda i, be, nv, nu: (be[i], 0, 0))],
        out_specs=pl.BlockSpec(memory_space=pl.ANY),
        scratch_shapes=[pltpu.VMEM((2, MOE_BLOCK * n_sub, LANES), F32), pltpu.VMEM((2, MOE_BLOCK * n_sub, LANES), F32),
                        pltpu.SemaphoreType.DMA((2,)), pltpu.SemaphoreType.DMA((2,))],
    )
    return pl.pallas_call(
        functools.partial(_expert_body, n_blk=n_blk),
        grid_spec=grid_spec,
        out_shape=jax.ShapeDtypeStruct((h.shape[0] * TOP_K, LANES), F32),
        compiler_params=_params(),
        name="moe_experts",
    )(blk_e, n_valid, n_used, slots, slots, h, wgu, bgu, wdn, bdn)


def _combine_body(gate_ref, x_ref, fg_ref, *refs):
    y_refs, o_ref = refs[:TOP_K], refs[TOP_K]
    acc = x_ref[...]
    gate = gate_ref[...]
    sub = acc.shape[1] // LANES
    for k in range(TOP_K):
        acc = acc + gate[:, k:k + 1] * _load_token_tiles(y_refs[k], sub)
    o_ref[...] = _rms(acc, fg_ref[...])


def _combine(gate, x, fg, y, tm):
    m, d = x.shape
    n_sub = d // LANES
    n_tiles = m // tm
    y_spec = lambda k: pl.BlockSpec((tm * n_sub, LANES), lambda i: (k * n_tiles + i, 0))
    return pl.pallas_call(
        _combine_body,
        grid=(n_tiles,),
        in_specs=[pl.BlockSpec((tm, LANES), lambda i: (i, 0)),
                  pl.BlockSpec((tm, d), lambda i: (i, 0)),
                  pl.BlockSpec((1, d), lambda i: (0, 0))] + [y_spec(k) for k in range(TOP_K)],
        out_specs=pl.BlockSpec((tm, d), lambda i: (i, 0)),
        out_shape=jax.ShapeDtypeStruct((m, d), F32),
        compiler_params=_params(),
        name="moe_combine",
    )(gate, x, fg, *([y] * TOP_K))


def _moe_final(x2d, w, tm):
    m, d = x2d.shape
    h, gate, sel = _router(x2d, w["norm_ffn_g"], w["router_w"], w["router_b"], tm)
    flat_e = sel[:, :TOP_K].T.reshape(-1)
    n_assign = m * TOP_K
    order = jnp.argsort(flat_e, stable=True).astype(jnp.int32)
    experts = jnp.arange(N_EXPERTS, dtype=jnp.int32)
    counts = jnp.sum((flat_e[:, None] == experts[None, :]).astype(jnp.int32), axis=0)
    padded = (counts + MOE_BLOCK - 1) // MOE_BLOCK * MOE_BLOCK
    pad_end = jnp.cumsum(padded)
    pad_start = pad_end - padded
    start = jnp.cumsum(counts) - counts
    n_blk = -(-(n_assign + N_EXPERTS * (MOE_BLOCK - 1)) // MOE_BLOCK)
    blk_row0 = jnp.arange(n_blk, dtype=jnp.int32) * MOE_BLOCK
    blk_e = jnp.minimum(jnp.sum((blk_row0[:, None] >= pad_end[None, :]).astype(jnp.int32), axis=1), N_EXPERTS - 1)
    n_used = (pad_end[-1:] // MOE_BLOCK).astype(jnp.int32)
    off = blk_row0 - pad_start[blk_e]
    n_valid = jnp.clip(counts[blk_e] - off, 0, MOE_BLOCK).astype(jnp.int32)
    within = jnp.arange(MOE_BLOCK, dtype=jnp.int32)[None, :]
    pos = jnp.clip((start[blk_e] + off)[:, None] + within, 0, n_assign - 1)
    slots = jnp.where(within < n_valid[:, None], order[pos], 0).astype(jnp.int32).reshape(n_blk, 1, MOE_BLOCK)
    y = _experts(h, slots, blk_e, n_valid, n_used, w["moe_w_gu"], w["moe_b_gu"], w["moe_w_down"], w["moe_b_down"])
    return _combine(gate, x2d, w["final_norm_g"], y, tm)


def _layer(x, shift_prev, s0, mk, mv, w):
    b, t, d = x.shape
    m = b * t
    tm = min(256, m)
    x2d = x.reshape(m, d)
    uv = _norm_matmul(x2d, w["norm_mix_g"], w["w_uv"], tm)
    rw = _norm_matmul(x2d, w["norm_mix_g"], w["w_rw"], tm)
    gates = _norm_matmul(x2d, w["norm_mix_g"], w["w_gates"], tm)
    cols = rw.shape[1]

    rows = min(t, GMLP_CHUNK)
    ws = w["gmlp_ws"][:, :rows, :rows]
    bs_full = jnp.repeat(w["gmlp_bs"][:, :rows].T, d // GMLP_GROUPS, axis=1)
    ya, v_n = _gmlp(uv, gates, w["gmlp_v_norm_g"], ws, bs_full, rows)

    n_heads = s0.shape[1]
    n_pairs = n_heads // 2
    tp = -(-t // RWKV_CHUNK) * RWKV_CHUNK
    pad3 = lambda z: jnp.pad(z.reshape(b, t, -1), ((0, 0), (0, tp - t), (0, 0)))
    st_t = jnp.swapaxes(s0, 2, 3).reshape(b, n_pairs, 2, RWKV_HEAD_DIM, RWKV_HEAD_DIM)
    st0 = jnp.zeros((b, n_pairs, LANES, LANES), F32)
    st0 = st0.at[:, :, :RWKV_HEAD_DIM, :RWKV_HEAD_DIM].set(st_t[:, :, 0])
    st0 = st0.at[:, :, RWKV_HEAD_DIM:, RWKV_HEAD_DIM:].set(st_t[:, :, 1])
    shift0 = jnp.broadcast_to(shift_prev, (b, 8, cols))
    merged, st_out, last = _rwkv(pad3(rw), pad3(gates), pad3(ya), shift0, st0, w, t)
    merged = merged[:, :t].reshape(m, d)
    s_new = jnp.stack([st_out[:, :, :RWKV_HEAD_DIM, :RWKV_HEAD_DIM], st_out[:, :, RWKV_HEAD_DIM:, RWKV_HEAD_DIM:]],
                      axis=2).reshape(b, n_heads, RWKV_HEAD_DIM, RWKV_HEAD_DIM)
    s_new = jnp.swapaxes(s_new, 2, 3)
    shift_new = last[:, :1, :]

    x1 = _matmul_res(merged, w["w_out"], x2d, tm)
    x2 = _cross_attn(x1.reshape(b, t, d), w["norm_ca_g"], w["ca_wq"], mk, mv, w["ca_wo"], min(512, t))
    y = _moe_final(x2.reshape(m, d), w, min(256, m))
    return y.reshape(b, t, d), shift_new, s_new, v_n.reshape(b, t, d)


def kernel(x_prompt, x_sample, mem_prompt, state_shift, state_rwkv, cache_mem_k, cache_mem_v, norm_mix_g, w_in, gmlp_v_norm_g, gmlp_ws, gmlp_bs, rwkv_mu, rwkv_w0, rwkv_w2, rwkv_a0, rwkv_a2, rwkv_g2, rwkv_k_k, rwkv_k_a, rwkv_r_k, rwkv_lnx_w, rwkv_lnx_b, w_out, norm_ca_g, norm_mem_g, ca_wq, ca_wk, ca_wv, ca_wo, norm_ffn_g, router_w, router_b, moe_w_gu, moe_b_gu, moe_w_down, moe_b_down, final_norm_g):
    depth = w_in.shape[0]
    assert depth == 1, "the fused MoE + final-norm tail assumes a single layer"
    l = 0
    bp, _, d = x_prompt.shape
    width = rwkv_w0.shape[1]
    n_heads = width // RWKV_HEAD_DIM
    gw = gmlp_v_norm_g.shape[1]
    row = lambda z: z.reshape(1, -1)
    zpad = jnp.zeros((LANES - rwkv_w2.shape[1], width), F32)
    w = dict(
        norm_mix_g=row(norm_mix_g[l]),
        w_uv=w_in[l][:, :2 * gw].astype(BF16),
        w_rw=w_in[l][:, 2 * gw:w_in.shape[2] - 2 * d].astype(BF16),
        w_gates=w_in[l][:, w_in.shape[2] - 2 * d:].astype(BF16),
        gmlp_v_norm_g=row(gmlp_v_norm_g[l]), gmlp_ws=gmlp_ws[l], gmlp_bs=gmlp_bs[l],
        mu=row(rwkv_mu[l]), w0=row(rwkv_w0[l]), a0=row(rwkv_a0[l]),
        w2=jnp.concatenate([rwkv_w2[l], zpad], axis=0).astype(BF16),
        a2=jnp.concatenate([zpad, rwkv_a2[l]], axis=0).astype(BF16),
        g2=rwkv_g2[l].astype(BF16),
        k_k=row(rwkv_k_k[l]), k_a=row(rwkv_k_a[l]), r_k=row(rwkv_r_k[l]),
        lnx_w=row(rwkv_lnx_w[l]), lnx_b=row(rwkv_lnx_b[l]),
        w_out=w_out[l].astype(BF16), norm_ca_g=row(norm_ca_g[l]),
        ca_wq=ca_wq[l].astype(BF16), ca_wo=ca_wo[l].astype(BF16),
        norm_ffn_g=row(norm_ffn_g[l]),
        router_w=jnp.pad(router_w[l], ((0, 0), (0, LANES - N_EXPERTS))),
        router_b=jnp.pad(row(router_b[l]), ((0, 0), (0, LANES - N_EXPERTS)), constant_values=-jnp.inf),
        moe_w_gu=moe_w_gu[l].astype(BF16), moe_b_gu=moe_b_gu[l][:, None, :],
        moe_w_down=moe_w_down[l].astype(BF16), moe_b_down=moe_b_down[l][:, None, :],
        final_norm_g=row(final_norm_g),
    )

    n_mem = mem_prompt.shape[1]
    mem2d = mem_prompt.reshape(bp * n_mem, d)
    mk = _norm_matmul(mem2d, row(norm_mem_g[l]), ca_wk[l].astype(BF16), 256).reshape(bp, n_mem, d)
    mv = _norm_matmul(mem2d, row(norm_mem_g[l]), ca_wv[l].astype(BF16), 256).reshape(bp, n_mem, d)
    shift0 = jnp.zeros((bp, 1, rwkv_mu.shape[1]), F32)
    s_zero = jnp.zeros((bp, n_heads, RWKV_HEAD_DIM, RWKV_HEAD_DIM), F32)
    y_p, sh_p, st_p, _ = _layer(x_prompt, shift0, s_zero, mk.astype(BF16), mv.astype(BF16), w)

    bs = x_sample.shape[0]
    cmk = cache_mem_k[l].reshape(bs, n_mem, d).astype(BF16)
    cmv = cache_mem_v[l].reshape(bs, n_mem, d).astype(BF16)
    y_s, sh_s, st_s, v_s = _layer(x_sample, state_shift[l], state_rwkv[l], cmk, cmv, w)

    ca_shape = (1, bp, n_mem, CA_HEADS, d // CA_HEADS)
    return (y_p, y_s, sh_p[None], st_p[None], mk.reshape(ca_shape), mv.reshape(ca_shape),
            sh_s[None], st_s[None], v_s[None])
```

```python
import functools

import jax
import jax.numpy as jnp
from jax import lax
from jax.experimental import pallas as pl
from jax.experimental.pallas import tpu as pltpu

F32 = jnp.float32
BF16 = jnp.bfloat16

NORM_EPS = 1e-5
GN_EPS = 64e-5
RWKV_HEAD_DIM = 64
RWKV_CHUNK = 64
RWKV_SEQS_PER_STEP = 2
DECAY_SCALE = 0.6065306597126334
GMLP_CHUNK = 128
GMLP_GROUPS = 8
CA_HEADS = 4
N_EXPERTS = 32
TOP_K = 4
SWIGLU_LIMIT = 7.0
SWIGLU_ALPHA = 1.702
LANES = 128
MOE_BLOCK = 256
ROW_UNROLL = 8
VMEM_LIMIT = 56 * 1024 * 1024


def _params(n_axes=1):
    return pltpu.CompilerParams(dimension_semantics=("arbitrary",) * n_axes, vmem_limit_bytes=VMEM_LIMIT)


def _rms(x, g):
    return x * lax.rsqrt(jnp.mean(x * x, axis=-1, keepdims=True) + NORM_EPS) * g


def _gelu(x):
    return x * (lax.erf(x * 0.7071067811865476) + 1.0) * 0.5


def _sigmoid(x):
    return 1.0 / (1.0 + jnp.exp(-x))


def _bdot(a, b):
    return jnp.dot(a.astype(BF16), b.astype(BF16), preferred_element_type=F32)


_DOT_DIMS = {"nn": (((1,), (0,)), ((), ())), "nt": (((1,), (1,)), ((), ())), "tn": (((0,), (0,)), ((), ()))}


def _dot1(a, b, mode="nn"):
    return lax.dot_general(a.astype(BF16), b.astype(BF16), _DOT_DIMS[mode], preferred_element_type=F32)


def _hi_lo(x):
    hi = x.astype(BF16)
    lo = (x - hi.astype(F32)).astype(BF16)
    return hi, lo


def _dot3(a, b, mode="nn"):
    ah, al = _hi_lo(a)
    bh, bl = _hi_lo(b)
    if mode == "nn":
        lhs = jnp.concatenate([ah, ah, al], axis=1)
        rhs = jnp.concatenate([bh, bl, bh], axis=0)
        dims = (((1,), (0,)), ((), ()))
    elif mode == "nt":
        lhs = jnp.concatenate([ah, ah, al], axis=1)
        rhs = jnp.concatenate([bh, bl, bh], axis=1)
        dims = (((1,), (1,)), ((), ()))
    else:
        lhs = jnp.concatenate([ah, ah, al], axis=0)
        rhs = jnp.concatenate([bh, bl, bh], axis=0)
        dims = (((0,), (0,)), ((), ()))
    return lax.dot_general(lhs, rhs, dims, preferred_element_type=F32)


def _dot2_exact_rhs(a, b_bf16):
    ah, al = _hi_lo(a)
    lhs = jnp.concatenate([ah, al], axis=1)
    rhs = jnp.concatenate([b_bf16, b_bf16], axis=0)
    return jnp.dot(lhs, rhs, preferred_element_type=F32)


def _dot2_exact_lhs(a_bf16, b):
    bh, bl = _hi_lo(b)
    lhs = jnp.concatenate([a_bf16, a_bf16], axis=1)
    rhs = jnp.concatenate([bh, bl], axis=0)
    return jnp.dot(lhs, rhs, preferred_element_type=F32)


def _norm_matmul_body(x_ref, g_ref, w_ref, o_ref):
    xn = _rms(x_ref[...], g_ref[...])
    o_ref[...] = jnp.dot(xn.astype(BF16), w_ref[...], preferred_element_type=F32)


def _norm_matmul(x, g, w, tm):
    m, k = x.shape
    n = w.shape[1]
    return pl.pallas_call(
        _norm_matmul_body,
        grid=(m // tm,),
        in_specs=[pl.BlockSpec((tm, k), lambda i: (i, 0)),
                  pl.BlockSpec((1, k), lambda i: (0, 0)),
                  pl.BlockSpec((k, n), lambda i: (0, 0))],
        out_specs=pl.BlockSpec((tm, n), lambda i: (i, 0)),
        out_shape=jax.ShapeDtypeStruct((m, n), F32),
        compiler_params=_params(),
        name="norm_matmul",
    )(x, g, w)


def _sum_matmul_res_body(a_ref, w_ref, r_ref, o_ref):
    o_ref[...] = r_ref[...] + jnp.dot(a_ref[...].astype(BF16), w_ref[...], preferred_element_type=F32)


def _matmul_res(a, w, res, tm):
    m, k = a.shape
    n = w.shape[1]
    return pl.pallas_call(
        _sum_matmul_res_body,
        grid=(m // tm,),
        in_specs=[pl.BlockSpec((tm, k), lambda i: (i, 0)),
                  pl.BlockSpec((k, n), lambda i: (0, 0)),
                  pl.BlockSpec((tm, n), lambda i: (i, 0))],
        out_specs=pl.BlockSpec((tm, n), lambda i: (i, 0)),
        out_shape=jax.ShapeDtypeStruct((m, n), F32),
        compiler_params=_params(),
        name="matmul_res",
    )(a, w, res)


def _gmlp_body(u_ref, v_ref, ga_ref, vg_ref, ws_ref, bs_ref, ya_ref, vn_ref, *, rows):
    gv = _gelu(v_ref[...])
    vn = _rms(gv, vg_ref[...])
    vn_ref[...] = vn
    ri = lax.broadcasted_iota(jnp.int32, (rows, rows), 0)
    ci = lax.broadcasted_iota(jnp.int32, (rows, rows), 1)
    tril = ri >= ci
    gu = _gelu(u_ref[...])
    gate = _sigmoid(ga_ref[...])
    bias = bs_ref[...]
    width = vn.shape[1] // GMLP_GROUPS
    for g in range(GMLP_GROUPS):
        sl = slice(g * width, (g + 1) * width)
        w = jnp.where(tril, ws_ref[g], 0.0)
        sp = _bdot(w, vn[:, sl]) + bias[:, sl]
        ya_ref[:, sl] = gate[:, sl] * (gu[:, sl] * sp)


def _gmlp(uv, gates, vg, ws, bs_full, rows):
    m = uv.shape[0]
    w = uv.shape[1] // 2
    return pl.pallas_call(
        functools.partial(_gmlp_body, rows=rows),
        grid=(m // rows,),
        in_specs=[pl.BlockSpec((rows, w), lambda i: (i, 0)),
                  pl.BlockSpec((rows, w), lambda i: (i, 1)),
                  pl.BlockSpec((rows, w), lambda i: (i, 0)),
                  pl.BlockSpec((1, w), lambda i: (0, 0)),
                  pl.BlockSpec((GMLP_GROUPS, rows, rows), lambda i: (0, 0, 0)),
                  pl.BlockSpec((rows, w), lambda i: (0, 0))],
        out_specs=[pl.BlockSpec((rows, w), lambda i: (i, 0)),
                   pl.BlockSpec((rows, w), lambda i: (i, 0))],
        out_shape=[jax.ShapeDtypeStruct((m, w), F32), jax.ShapeDtypeStruct((m, w), F32)],
        compiler_params=_params(),
        name="gmlp",
    )(uv, uv, gates, vg, ws, bs_full)


def _rwkv_pair(r, kf, v, kkr, a, logd, st2, rk, lnw, lnb, consts):
    m0b, m1b, strict, incl, bd, bd_ones, tril_ones, n_levels = consts
    chunk = r.shape[0]

    def stack(z):
        zb = z.astype(BF16)
        return jnp.concatenate([zb * m0b, zb * m1b], axis=0)

    ss = _dot2_exact_rhs(kkr * kkr, bd_ones)
    cl = _dot2_exact_lhs(tril_ones, logd)
    bonus = _dot1(r * kf * rk, bd_ones) * v
    yield
    kk = kkr / jnp.maximum(jnp.sqrt(ss), 1e-12)
    cl_end = cl[chunk - 1:chunk, :]
    d_in = jnp.exp(cl)
    d_inv = jnp.exp(-cl)
    a_t = -(kk * jnp.exp(cl - logd))
    kb = kk * a
    b_t = kb * d_inv
    k_t = kf * d_inv
    r_t = r * d_in
    d_tail = jnp.exp(cl_end - cl)
    d_end = jnp.exp(cl_end)

    g = _dot1(jnp.concatenate([a_t, r_t], axis=0),
              jnp.concatenate([stack(b_t), stack(k_t)], axis=0), mode="nt")
    yield
    n_cat = jnp.where(strict, g[:chunk, :LANES], 0.0)
    a_ak = jnp.where(strict, g[:chunk, LANES:], 0.0)
    p_cat = jnp.where(incl, g[chunk:, :LANES], 0.0)
    q_cat = jnp.where(incl, g[chunk:, LANES:], 0.0)

    v_st = stack(v)
    st2b = st2.astype(BF16)
    x = _dot1(jnp.concatenate([a_t, a_ak], axis=1), jnp.concatenate([st2b, v_st], axis=0))
    yield
    n_pow = n_cat
    for lvl in range(n_levels):
        if lvl < n_levels - 1:
            z = _dot1(n_pow, jnp.concatenate([stack(x), stack(n_pow)], axis=1))
            yield
            x = x + z[:, :LANES]
            n_pow = z[:, LANES:]
        else:
            z = _dot1(n_pow, stack(x))
            yield
            x = x + z
    u = x

    o = _dot1(jnp.concatenate([r_t, p_cat, q_cat], axis=1),
              jnp.concatenate([st2b, stack(u), v_st], axis=0))
    st_add = _dot3(jnp.concatenate([kb * d_tail, kf * d_tail], axis=0), jnp.concatenate([u, v], axis=0), mode="tn")
    yield
    d_end_col = jnp.transpose(jnp.broadcast_to(d_end, (LANES, LANES)))
    st_new = st2 * d_end_col + jnp.where(bd, st_add, 0.0)

    inv_n = 1.0 / RWKV_HEAD_DIM
    mu = _dot1(o, bd_ones) * inv_n
    yield
    dlt = o - mu
    var = _dot1(dlt * dlt, bd_ones) * inv_n
    yield
    on = dlt * lax.rsqrt(var + GN_EPS) * lnw + lnb
    return on + bonus, st_new


def _interleave(generators):
    results = [None] * len(generators)
    live = list(range(len(generators)))
    while live:
        still = []
        for i in live:
            try:
                next(generators[i])
                still.append(i)
            except StopIteration as stop:
                results[i] = stop.value
        live = still
    return results


def _rwkv_body(rw_ref, gb_ref, ya_ref, sh0_ref, st0_ref, mu_ref, w0_ref, a0_ref, w2_ref, a2_ref, g2_ref,
               kk_ref, ka_ref, rk_ref, lnw_ref, lnb_ref, out_ref, stout_ref, last_ref, st_s, carry_s,
               *, chunk, t_valid, n_chunks):
    c = pl.program_id(1)
    n_seq, _, width = out_ref.shape
    n_pairs = width // LANES
    cols = rw_ref.shape[2]

    @pl.when(c == 0)
    def _init():
        st_s[...] = st0_ref[...]
        carry_s[...] = sh0_ref[...]

    def prepare(q):
        rw = rw_ref[q]
        row_w = lax.broadcasted_iota(jnp.int32, (chunk, cols), 0)
        rw_prev = jnp.where(row_w == 0, carry_s[q, 0:1, :], pltpu.roll(rw, 1, axis=0))
        last_row = rw[(t_valid - 1) % chunk:(t_valid - 1) % chunk + 1, :]
        carry_s[q, 0:1, :] = last_row
        last_ref[q] = jnp.broadcast_to(last_row, (8, cols))
        mix = rw + (rw_prev - rw) * mu_ref[...]

        r = mix[:, 0:width]
        k = mix[:, width:2 * width]
        v = mix[:, 2 * width:3 * width]
        wa = mix[:, 3 * width:3 * width + LANES]
        gl = mix[:, 3 * width + LANES:3 * width + 2 * LANES]
        logd = -DECAY_SCALE * _sigmoid(w0_ref[...] + _bdot(jnp.tanh(wa), w2_ref[...]))
        a = _sigmoid(a0_ref[...] + _bdot(wa, a2_ref[...]))
        gate_g = _bdot(_sigmoid(gl), g2_ref[...])
        kkr = k * kk_ref[...]
        kf = k * (1.0 + (a - 1.0) * ka_ref[...])
        if t_valid < chunk * n_chunks:
            valid = lax.broadcasted_iota(jnp.int32, (chunk, width), 0) + c * chunk < t_valid
            logd = jnp.where(valid, logd, 0.0)
            kkr = jnp.where(valid, kkr, 0.0)
            kf = jnp.where(valid, kf, 0.0)
            v = jnp.where(valid, v, 0.0)
        return r, kf, v, kkr, a, logd, gate_g

    lane = lax.broadcasted_iota(jnp.int32, (chunk, LANES), 1)
    rowi = lax.broadcasted_iota(jnp.int32, (chunk, LANES), 0)
    m0 = lane < RWKV_HEAD_DIM
    lane_in_head = jnp.where(m0, lane, lane - RWKV_HEAD_DIM)
    r2 = lax.broadcasted_iota(jnp.int32, (LANES, LANES), 0)
    c2 = lax.broadcasted_iota(jnp.int32, (LANES, LANES), 1)
    bd = (r2 < RWKV_HEAD_DIM) == (c2 < RWKV_HEAD_DIM)
    rl = lax.broadcasted_iota(jnp.int32, (chunk, chunk), 0)
    cl_ = lax.broadcasted_iota(jnp.int32, (chunk, chunk), 1)
    consts = (jnp.where(m0, 1.0, 0.0).astype(BF16), jnp.where(m0, 0.0, 1.0).astype(BF16),
              rowi > lane_in_head, rowi >= lane_in_head, bd,
              jnp.where(bd, 1.0, 0.0).astype(BF16), jnp.where(rl >= cl_, 1.0, 0.0).astype(BF16),
              chunk.bit_length() - 1)

    rk = rk_ref[...]
    lnw = lnw_ref[...]
    lnb = lnb_ref[...]
    gens, gate_gs = [], []
    for q in range(n_seq):
        r, kf, v, kkr, a, logd, gate_g = prepare(q)
        gate_gs.append(gate_g)
        for p in range(n_pairs):
            sl = slice(p * LANES, (p + 1) * LANES)
            gens.append(_rwkv_pair(r[:, sl], kf[:, sl], v[:, sl], kkr[:, sl], a[:, sl], logd[:, sl], st_s[q, p],
                                   rk[:, sl], lnw[:, sl], lnb[:, sl], consts))
    results = _interleave(gens)
    for q in range(n_seq):
        gate_b = _sigmoid(gb_ref[q])
        ya = ya_ref[q]
        for p in range(n_pairs):
            sl = slice(p * LANES, (p + 1) * LANES)
            y, st_new = results[q * n_pairs + p]
            st_s[q, p] = st_new
            out_ref[q, :, sl] = ya[:, sl] + gate_b[:, sl] * (y * gate_gs[q][:, sl])

    @pl.when(c == n_chunks - 1)
    def _fin():
        stout_ref[...] = st_s[...]


def _rwkv(rw, gates, ya, shift0, st0, p, t_valid):
    b, tp, cols = rw.shape
    width = ya.shape[2]
    n_pairs = width // LANES
    chunk = RWKV_CHUNK
    n_chunks = tp // chunk
    n_seq = RWKV_SEQS_PER_STEP
    assert b % n_seq == 0
    vec = lambda n: pl.BlockSpec((1, n), lambda i, j: (0, 0))
    lora = lambda n: pl.BlockSpec((n, width), lambda i, j: (0, 0))
    return pl.pallas_call(
        functools.partial(_rwkv_body, chunk=chunk, t_valid=t_valid, n_chunks=n_chunks),
        grid=(b // n_seq, n_chunks),
        in_specs=[pl.BlockSpec((n_seq, chunk, cols), lambda i, j: (i, j, 0)),
                  pl.BlockSpec((n_seq, chunk, width), lambda i, j: (i, j, 1)),
                  pl.BlockSpec((n_seq, chunk, width), lambda i, j: (i, j, 0)),
                  pl.BlockSpec((n_seq, 8, cols), lambda i, j: (i, 0, 0)),
                  pl.BlockSpec((n_seq, n_pairs, LANES, LANES), lambda i, j: (i, 0, 0, 0)),
                  vec(cols), vec(width), vec(width), lora(LANES), lora(LANES), lora(LANES),
                  vec(width), vec(width), vec(width), vec(width), vec(width)],
        out_specs=[pl.BlockSpec((n_seq, chunk, width), lambda i, j: (i, j, 0)),
                   pl.BlockSpec((n_seq, n_pairs, LANES, LANES), lambda i, j: (i, 0, 0, 0)),
                   pl.BlockSpec((n_seq, 8, cols), lambda i, j: (i, 0, 0))],
        out_shape=[jax.ShapeDtypeStruct((b, tp, width), F32),
                   jax.ShapeDtypeStruct((b, n_pairs, LANES, LANES), F32),
                   jax.ShapeDtypeStruct((b, 8, cols), F32)],
        scratch_shapes=[pltpu.VMEM((n_seq, n_pairs, LANES, LANES), F32), pltpu.VMEM((n_seq, 8, cols), F32)],
        compiler_params=_params(2),
        name="rwkv7",
    )(rw, gates, ya, shift0, st0, p["mu"], p["w0"], p["a0"], p["w2"], p["a2"], p["g2"],
      p["k_k"], p["k_a"], p["r_k"], p["lnx_w"], p["lnx_b"])


def _attn_body(x_ref, g_ref, wq_ref, mk_ref, mv_ref, wo_ref, o_ref):
    x = x_ref[0]
    q = jnp.dot(_rms(x, g_ref[...]).astype(BF16), wq_ref[...], preferred_element_type=F32)
    mk = mk_ref[0]
    mv = mv_ref[0]
    hd = x.shape[1] // CA_HEADS
    outs = []
    for h in range(CA_HEADS):
        sl = slice(h * hd, (h + 1) * hd)
        s = lax.dot_general(q[:, sl].astype(BF16), mk[:, sl], (((1,), (1,)), ((), ())),
                            preferred_element_type=F32) * (hd ** -0.5)
        e = jnp.exp(s - jnp.max(s, axis=-1, keepdims=True))
        pr = e / jnp.sum(e, axis=-1, keepdims=True)
        outs.append(jnp.dot(pr.astype(BF16), mv[:, sl], preferred_element_type=F32))
    o = jnp.concatenate(outs, axis=-1)
    o_ref[0] = x + jnp.dot(o.astype(BF16), wo_ref[...], preferred_element_type=F32)


def _cross_attn(x, g, wq, mk, mv, wo, tm):
    b, t, d = x.shape
    n_mem = mk.shape[1]
    return pl.pallas_call(
        _attn_body,
        grid=(b, t // tm),
        in_specs=[pl.BlockSpec((1, tm, d), lambda i, j: (i, j, 0)),
                  pl.BlockSpec((1, d), lambda i, j: (0, 0)),
                  pl.BlockSpec((d, d), lambda i, j: (0, 0)),
                  pl.BlockSpec((1, n_mem, d), lambda i, j: (i, 0, 0)),
                  pl.BlockSpec((1, n_mem, d), lambda i, j: (i, 0, 0)),
                  pl.BlockSpec((d, d), lambda i, j: (0, 0))],
        out_specs=pl.BlockSpec((1, tm, d), lambda i, j: (i, j, 0)),
        out_shape=jax.ShapeDtypeStruct((b, t, d), F32),
        compiler_params=_params(2),
        name="cross_attn",
    )(x, g, wq, mk, mv, wo)


def _store_token_tiles(ref, rows):
    n, d = rows.shape
    sub = d // LANES
    for s in range(sub):
        ref[pl.ds(s, n, stride=sub), :] = rows[:, s * LANES:(s + 1) * LANES]


def _load_token_tiles(ref, sub):
    n = ref.shape[0] // sub
    return jnp.concatenate([ref[pl.ds(s, n, stride=sub), :] for s in range(sub)], axis=1)


def _router_body(x_ref, g_ref, wr_ref, br_ref, h_ref, gate_ref, idx_ref):
    h = _rms(x_ref[...], g_ref[...])
    _store_token_tiles(h_ref, h)
    logits = _dot3(h, wr_ref[...]) + br_ref[...]
    lane = lax.broadcasted_iota(jnp.int32, logits.shape, 1)
    neg = jnp.float32(-jnp.inf)
    vals, idxs = [], []
    for _ in range(TOP_K):
        m = jnp.max(logits, axis=-1, keepdims=True)
        idx = jnp.min(jnp.where(logits == m, lane, LANES), axis=-1, keepdims=True)
        vals.append(m)
        idxs.append(idx)
        logits = jnp.where(lane == idx, neg, logits)
    es = [jnp.exp(vv - vals[0]) for vv in vals]
    den = es[0] + es[1] + es[2] + es[3]
    gate = jnp.zeros(logits.shape, F32)
    sel = jnp.zeros(logits.shape, jnp.int32)
    for j in range(TOP_K):
        gate = jnp.where(lane == j, es[j] / den, gate)
        sel = jnp.where(lane == j, idxs[j], sel)
    gate_ref[...] = gate
    idx_ref[...] = sel


def _router(x, g, wr_pad, br_pad, tm):
    m, d = x.shape
    return pl.pallas_call(
        _router_body,
        grid=(m // tm,),
        in_specs=[pl.BlockSpec((tm, d), lambda i: (i, 0)),
                  pl.BlockSpec((1, d), lambda i: (0, 0)),
                  pl.BlockSpec((d, LANES), lambda i: (0, 0)),
                  pl.BlockSpec((1, LANES), lambda i: (0, 0))],
        out_specs=[pl.BlockSpec((tm * d // LANES, LANES), lambda i: (i, 0)),
                   pl.BlockSpec((tm, LANES), lambda i: (i, 0)),
                   pl.BlockSpec((tm, LANES), lambda i: (i, 0))],
        out_shape=[jax.ShapeDtypeStruct((m * d // LANES, LANES), F32),
                   jax.ShapeDtypeStruct((m, LANES), F32),
                   jax.ShapeDtypeStruct((m, LANES), jnp.int32)],
        compiler_params=_params(),
        name="moe_router",
    )(x, g, wr_pad, br_pad)


def _wait_rows(make_copy, n_rows):
    for bit in reversed(range(MOE_BLOCK.bit_length())):
        size = 1 << bit

        @pl.when((n_rows & size) != 0)
        def _wait(size=size):
            make_copy(size).wait()


def _expert_body(blk_e_ref, n_valid_ref, n_used_ref, slot_ref, slot_next_ref, h_ref, wgu_ref, bgu_ref,
                 wdn_ref, bdn_ref, y_ref, xbuf, ybuf, gsem, ssem, *, n_blk):
    del blk_e_ref
    i = pl.program_id(0)
    n_used = n_used_ref[0]
    cur = lax.rem(i, 2)
    f = wdn_ref.shape[1]
    sub = wdn_ref.shape[2] // LANES
    m = h_ref.shape[0] // sub

    def token_of(slot):
        return (slot & (m - 1)) if m & (m - 1) == 0 else lax.rem(slot, m)

    def tiles(first, count=1):
        start = first * sub if isinstance(first, int) else pl.multiple_of(first * sub, sub)
        return pl.ds(start, count * sub)

    def for_rows(n_rows, row_fn):
        @pl.when(n_rows == MOE_BLOCK)
        def _full():
            def body(g, carry):
                for u in range(ROW_UNROLL):
                    row_fn(g * ROW_UNROLL + u)
                return carry
            lax.fori_loop(0, MOE_BLOCK // ROW_UNROLL, body, 0)

        @pl.when(n_rows != MOE_BLOCK)
        def _partial():
            def body(r, carry):
                row_fn(r)
                return carry
            lax.fori_loop(0, n_rows, body, 0)

    def gather_start(slots, n_rows, b):
        def row(r):
            pltpu.make_async_copy(h_ref.at[tiles(token_of(slots[0, 0, r]))], xbuf.at[b, tiles(r)],
                                  gsem.at[b]).start()
        for_rows(n_rows, row)

    def gather_wait(n_rows, b):
        _wait_rows(lambda k: pltpu.make_async_copy(h_ref.at[tiles(0, k)], xbuf.at[b, tiles(0, k)], gsem.at[b]),
                   n_rows)

    def scatter_start(slots, n_rows, b):
        def row(r):
            pltpu.make_async_copy(ybuf.at[b, tiles(r)], y_ref.at[tiles(slots[0, 0, r])], ssem.at[b]).start()
        for_rows(n_rows, row)

    def scatter_wait(n_rows, b):
        _wait_rows(lambda k: pltpu.make_async_copy(ybuf.at[b, tiles(0, k)], y_ref.at[tiles(0, k)], ssem.at[b]),
                   n_rows)

    @pl.when(i == 0)
    def _first():
        xbuf[...] = jnp.zeros(xbuf.shape, F32)
        gather_start(slot_ref, n_valid_ref[0], 0)

    @pl.when(i + 1 < n_used)
    def _prefetch():
        gather_start(slot_next_ref, n_valid_ref[i + 1], 1 - cur)

    @pl.when(i < n_used)
    def _run():
        gather_wait(n_valid_ref[i], cur)

        @pl.when(i >= 2)
        def _free_ybuf():
            scatter_wait(n_valid_ref[i - 2], cur)

        x = _load_token_tiles(xbuf.at[cur], sub)
        hh = jnp.dot(x.astype(BF16), wgu_ref[0], preferred_element_type=F32) + bgu_ref[0]
        hg = jnp.minimum(hh[:, :f], SWIGLU_LIMIT)
        hl = jnp.clip(hh[:, f:], -SWIGLU_LIMIT, SWIGLU_LIMIT)
        act = hg * _sigmoid(SWIGLU_ALPHA * hg) * (hl + 1.0)
        _store_token_tiles(ybuf.at[cur], jnp.dot(act.astype(BF16), wdn_ref[0], preferred_element_type=F32)
                           + bdn_ref[0])
        scatter_start(slot_ref, n_valid_ref[i], cur)

    @pl.when(i == n_blk - 1)
    def _drain():
        last = n_used - 1

        @pl.when(last >= 1)
        def _prev():
            scatter_wait(n_valid_ref[last - 1], lax.rem(last - 1, 2))

        @pl.when(last >= 0)
        def _last():
            scatter_wait(n_valid_ref[last], lax.rem(last, 2))


def _experts(h, slots, blk_e, n_valid, n_used, wgu, bgu, wdn, bdn):
    f, d = wdn.shape[1:]
    n_sub = d // LANES
    n_blk = slots.shape[0]
    grid_spec = pltpu.PrefetchScalarGridSpec(
        num_scalar_prefetch=3,
        grid=(n_blk,),
        in_specs=[pl.BlockSpec((1, 1, MOE_BLOCK), lambda i, be, nv, nu: (i, 0, 0), memory_space=pltpu.SMEM),
                  pl.BlockSpec((1, 1, MOE_BLOCK), lambda i, be, nv, nu: (jnp.minimum(i + 1, n_blk - 1), 0, 0),
                               memory_space=pltpu.SMEM),
                  pl.BlockSpec(memory_space=pl.ANY),
                  pl.BlockSpec((1, d, 2 * f), lambda i, be, nv, nu: (be[i], 0, 0)),
                  pl.BlockSpec((1, 1, 2 * f), lambda i, be, nv, nu: (be[i], 0, 0)),
                  pl.BlockSpec((1, f, d), lambda i, be, nv, nu: (be[i], 0, 0)),
                  pl.BlockSpec((1, 1, d), lambda i, be, nv, nu: (be[i], 0, 0))],
        out_specs=pl.BlockSpec(memory_space=pl.ANY),
        scratch_shapes=[pltpu.VMEM((2, MOE_BLOCK * n_sub, LANES), F32), pltpu.VMEM((2, MOE_BLOCK * n_sub, LANES), F32),
                        pltpu.SemaphoreType.DMA((2,)), pltpu.SemaphoreType.DMA((2,))],
    )
    return pl.pallas_call(
        functools.partial(_expert_body, n_blk=n_blk),
        grid_spec=grid_spec,
        out_shape=jax.ShapeDtypeStruct((h.shape[0] * TOP_K, LANES), F32),
        compiler_params=_params(),
        name="moe_experts",
    )(blk_e, n_valid, n_used, slots, slots, h, wgu, bgu, wdn, bdn)


def _combine_body(gate_ref, x_ref, fg_ref, *refs):
    y_refs, o_ref = refs[:TOP_K], refs[TOP_K]
    acc = x_ref[...]
    gate = gate_ref[...]
    sub = acc.shape[1] // LANES
    for k in range(TOP_K):
        acc = acc + gate[:, k:k + 1] * _load_token_tiles(y_refs[k], sub)
    o_ref[...] = _rms(acc, fg_ref[...])


def _combine(gate, x, fg, y, tm):
    m, d = x.shape
    n_sub = d // LANES
    n_tiles = m // tm
    y_spec = lambda k: pl.BlockSpec((tm * n_sub, LANES), lambda i: (k * n_tiles + i, 0))
    return pl.pallas_call(
        _combine_body,
        grid=(n_tiles,),
        in_specs=[pl.BlockSpec((tm, LANES), lambda i: (i, 0)),
                  pl.BlockSpec((tm, d), lambda i: (i, 0)),
                  pl.BlockSpec((1, d), lambda i: (0, 0))] + [y_spec(k) for k in range(TOP_K)],
        out_specs=pl.BlockSpec((tm, d), lambda i: (i, 0)),
        out_shape=jax.ShapeDtypeStruct((m, d), F32),
        compiler_params=_params(),
        name="moe_combine",
    )(gate, x, fg, *([y] * TOP_K))


def _moe_final(x2d, w, tm):
    m, d = x2d.shape
    h, gate, sel = _router(x2d, w["norm_ffn_g"], w["router_w"], w["router_b"], tm)
    flat_e = sel[:, :TOP_K].T.reshape(-1)
    n_assign = m * TOP_K
    order = jnp.argsort(flat_e, stable=True).astype(jnp.int32)
    experts = jnp.arange(N_EXPERTS, dtype=jnp.int32)
    counts = jnp.sum((flat_e[:, None] == experts[None, :]).astype(jnp.int32), axis=0)
    padded = (counts + MOE_BLOCK - 1) // MOE_BLOCK * MOE_BLOCK
    pad_end = jnp.cumsum(padded)
    pad_start = pad_end - padded
    start = jnp.cumsum(counts) - counts
    n_blk = -(-(n_assign + N_EXPERTS * (MOE_BLOCK - 1)) // MOE_BLOCK)
    blk_row0 = jnp.arange(n_blk, dtype=jnp.int32) * MOE_BLOCK
    blk_e = jnp.minimum(jnp.sum((blk_row0[:, None] >= pad_end[None, :]).astype(jnp.int32), axis=1), N_EXPERTS - 1)
    n_used = (pad_end[-1:] // MOE_BLOCK).astype(jnp.int32)
    off = blk_row0 - pad_start[blk_e]
    n_valid = jnp.clip(counts[blk_e] - off, 0, MOE_BLOCK).astype(jnp.int32)
    within = jnp.arange(MOE_BLOCK, dtype=jnp.int32)[None, :]
    pos = jnp.clip((start[blk_e] + off)[:, None] + within, 0, n_assign - 1)
    slots = jnp.where(within < n_valid[:, None], order[pos], 0).astype(jnp.int32).reshape(n_blk, 1, MOE_BLOCK)
    y = _experts(h, slots, blk_e, n_valid, n_used, w["moe_w_gu"], w["moe_b_gu"], w["moe_w_down"], w["moe_b_down"])
    return _combine(gate, x2d, w["final_norm_g"], y, tm)


def _layer(x, shift_prev, s0, mk, mv, w):
    b, t, d = x.shape
    m = b * t
    tm = min(256, m)
    x2d = x.reshape(m, d)
    uv = _norm_matmul(x2d, w["norm_mix_g"], w["w_uv"], tm)
    rw = _norm_matmul(x2d, w["norm_mix_g"], w["w_rw"], tm)
    gates = _norm_matmul(x2d, w["norm_mix_g"], w["w_gates"], tm)
    cols = rw.shape[1]

    rows = min(t, GMLP_CHUNK)
    ws = w["gmlp_ws"][:, :rows, :rows]
    bs_full = jnp.repeat(w["gmlp_bs"][:, :rows].T, d // GMLP_GROUPS, axis=1)
    ya, v_n = _gmlp(uv, gates, w["gmlp_v_norm_g"], ws, bs_full, rows)

    n_heads = s0.shape[1]
    n_pairs = n_heads // 2
    tp = -(-t // RWKV_CHUNK) * RWKV_CHUNK
    pad3 = lambda z: jnp.pad(z.reshape(b, t, -1), ((0, 0), (0, tp - t), (0, 0)))
    st_t = jnp.swapaxes(s0, 2, 3).reshape(b, n_pairs, 2, RWKV_HEAD_DIM, RWKV_HEAD_DIM)
    st0 = jnp.zeros((b, n_pairs, LANES, LANES), F32)
    st0 = st0.at[:, :, :RWKV_HEAD_DIM, :RWKV_HEAD_DIM].set(st_t[:, :, 0])
    st0 = st0.at[:, :, RWKV_HEAD_DIM:, RWKV_HEAD_DIM:].set(st_t[:, :, 1])
    shift0 = jnp.broadcast_to(shift_prev, (b, 8, cols))
    merged, st_out, last = _rwkv(pad3(rw), pad3(gates), pad3(ya), shift0, st0, w, t)
    merged = merged[:, :t].reshape(m, d)
    s_new = jnp.stack([st_out[:, :, :RWKV_HEAD_DIM, :RWKV_HEAD_DIM], st_out[:, :, RWKV_HEAD_DIM:, RWKV_HEAD_DIM:]],
                      axis=2).reshape(b, n_heads, RWKV_HEAD_DIM, RWKV_HEAD_DIM)
    s_new = jnp.swapaxes(s_new, 2, 3)
    shift_new = last[:, :1, :]

    x1 = _matmul_res(merged, w["w_out"], x2d, tm)
    x2 = _cross_attn(x1.reshape(b, t, d), w["norm_ca_g"], w["ca_wq"], mk, mv, w["ca_wo"], min(512, t))
    y = _moe_final(x2.reshape(m, d), w, min(256, m))
    return y.reshape(b, t, d), shift_new, s_new, v_n.reshape(b, t, d)


def kernel(x_prompt, x_sample, mem_prompt, state_shift, state_rwkv, cache_mem_k, cache_mem_v, norm_mix_g, w_in, gmlp_v_norm_g, gmlp_ws, gmlp_bs, rwkv_mu, rwkv_w0, rwkv_w2, rwkv_a0, rwkv_a2, rwkv_g2, rwkv_k_k, rwkv_k_a, rwkv_r_k, rwkv_lnx_w, rwkv_lnx_b, w_out, norm_ca_g, norm_mem_g, ca_wq, ca_wk, ca_wv, ca_wo, norm_ffn_g, router_w, router_b, moe_w_gu, moe_b_gu, moe_w_down, moe_b_down, final_norm_g):
    depth = w_in.shape[0]
    assert depth == 1, "the fused MoE + final-norm tail assumes a single layer"
    l = 0
    bp, _, d = x_prompt.shape
    width = rwkv_w0.shape[1]
    n_heads = width // RWKV_HEAD_DIM
    gw = gmlp_v_norm_g.shape[1]
    row = lambda z: z.reshape(1, -1)
    zpad = jnp.zeros((LANES - rwkv_w2.shape[1], width), F32)
    w = dict(
        norm_mix_g=row(norm_mix_g[l]),
        w_uv=w_in[l][:, :2 * gw].astype(BF16),
        w_rw=w_in[l][:, 2 * gw:w_in.shape[2] - 2 * d].astype(BF16),
        w_gates=w_in[l][:, w_in.shape[2] - 2 * d:].astype(BF16),
        gmlp_v_norm_g=row(gmlp_v_norm_g[l]), gmlp_ws=gmlp_ws[l], gmlp_bs=gmlp_bs[l],
        mu=row(rwkv_mu[l]), w0=row(rwkv_w0[l]), a0=row(rwkv_a0[l]),
        w2=jnp.concatenate([rwkv_w2[l], zpad], axis=0).astype(BF16),
        a2=jnp.concatenate([zpad, rwkv_a2[l]], axis=0).astype(BF16),
        g2=rwkv_g2[l].astype(BF16),
        k_k=row(rwkv_k_k[l]), k_a=row(rwkv_k_a[l]), r_k=row(rwkv_r_k[l]),
        lnx_w=row(rwkv_lnx_w[l]), lnx_b=row(rwkv_lnx_b[l]),
        w_out=w_out[l].astype(BF16), norm_ca_g=row(norm_ca_g[l]),
        ca_wq=ca_wq[l].astype(BF16), ca_wo=ca_wo[l].astype(BF16),
        norm_ffn_g=row(norm_ffn_g[l]),
        router_w=jnp.pad(router_w[l], ((0, 0), (0, LANES - N_EXPERTS))),
        router_b=jnp.pad(row(router_b[l]), ((0, 0), (0, LANES - N_EXPERTS)), constant_values=-jnp.inf),
        moe_w_gu=moe_w_gu[l].astype(BF16), moe_b_gu=moe_b_gu[l][:, None, :],
        moe_w_down=moe_w_down[l].astype(BF16), moe_b_down=moe_b_down[l][:, None, :],
        final_norm_g=row(final_norm_g),
    )

    n_mem = mem_prompt.shape[1]
    mem2d = mem_prompt.reshape(bp * n_mem, d)
    mk = _norm_matmul(mem2d, row(norm_mem_g[l]), ca_wk[l].astype(BF16), 256).reshape(bp, n_mem, d)
    mv = _norm_matmul(mem2d, row(norm_mem_g[l]), ca_wv[l].astype(BF16), 256).reshape(bp, n_mem, d)
    shift0 = jnp.zeros((bp, 1, rwkv_mu.shape[1]), F32)
    s_zero = jnp.zeros((bp, n_heads, RWKV_HEAD_DIM, RWKV_HEAD_DIM), F32)
    y_p, sh_p, st_p, _ = _layer(x_prompt, shift0, s_zero, mk.astype(BF16), mv.astype(BF16), w)

    bs = x_sample.shape[0]
    cmk = cache_mem_k[l].reshape(bs, n_mem, d).astype(BF16)
    cmv = cache_mem_v[l].reshape(bs, n_mem, d).astype(BF16)
    y_s, sh_s, st_s, v_s = _layer(x_sample, state_shift[l], state_rwkv[l], cmk, cmv, w)

    ca_shape = (1, bp, n_mem, CA_HEADS, d // CA_HEADS)
    return (y_p, y_s, sh_p[None], st_p[None], mk.reshape(ca_shape), mv.reshape(ca_shape),
            sh_s[None], st_s[None], v_s[None])
```

```python
import functools

import jax
import jax.numpy as jnp
from jax import lax
from jax.experimental import pallas as pl
from jax.experimental.pallas import tpu as pltpu

F32 = jnp.float32
BF16 = jnp.bfloat16

NORM_EPS = 1e-5
GN_EPS = 64e-5
RWKV_HEAD_DIM = 64
RWKV_CHUNK = 64
RWKV_SEQS_PER_STEP = 2
DECAY_SCALE = 0.6065306597126334
GMLP_CHUNK = 128
GMLP_GROUPS = 8
CA_HEADS = 4
N_EXPERTS = 32
TOP_K = 4
SWIGLU_LIMIT = 7.0
SWIGLU_ALPHA = 1.702
LANES = 128
MOE_BLOCK = 256
ROW_UNROLL = 8
VMEM_LIMIT = 56 * 1024 * 1024


def _params(n_axes=1):
    return pltpu.CompilerParams(dimension_semantics=("arbitrary",) * n_axes, vmem_limit_bytes=VMEM_LIMIT)


def _rms(x, g):
    return x * lax.rsqrt(jnp.mean(x * x, axis=-1, keepdims=True) + NORM_EPS) * g


def _gelu(x):
    return x * (lax.erf(x * 0.7071067811865476) + 1.0) * 0.5


def _sigmoid(x):
    return 1.0 / (1.0 + jnp.exp(-x))


def _bdot(a, b):
    return jnp.dot(a.astype(BF16), b.astype(BF16), preferred_element_type=F32)


_DOT_DIMS = {"nn": (((1,), (0,)), ((), ())), "nt": (((1,), (1,)), ((), ())), "tn": (((0,), (0,)), ((), ()))}


def _dot1(a, b, mode="nn"):
    return lax.dot_general(a.astype(BF16), b.astype(BF16), _DOT_DIMS[mode], preferred_element_type=F32)


def _hi_lo(x):
    hi = x.astype(BF16)
    lo = (x - hi.astype(F32)).astype(BF16)
    return hi, lo


def _dot3(a, b, mode="nn"):
    ah, al = _hi_lo(a)
    bh, bl = _hi_lo(b)
    if mode == "nn":
        lhs = jnp.concatenate([ah, ah, al], axis=1)
        rhs = jnp.concatenate([bh, bl, bh], axis=0)
        dims = (((1,), (0,)), ((), ()))
    elif mode == "nt":
        lhs = jnp.concatenate([ah, ah, al], axis=1)
        rhs = jnp.concatenate([bh, bl, bh], axis=1)
        dims = (((1,), (1,)), ((), ()))
    else:
        lhs = jnp.concatenate([ah, ah, al], axis=0)
        rhs = jnp.concatenate([bh, bl, bh], axis=0)
        dims = (((0,), (0,)), ((), ()))
    return lax.dot_general(lhs, rhs, dims, preferred_element_type=F32)


def _dot2_exact_rhs(a, b_bf16):
    ah, al = _hi_lo(a)
    lhs = jnp.concatenate([ah, al], axis=1)
    rhs = jnp.concatenate([b_bf16, b_bf16], axis=0)
    return jnp.dot(lhs, rhs, preferred_element_type=F32)


def _dot2_exact_lhs(a_bf16, b):
    bh, bl = _hi_lo(b)
    lhs = jnp.concatenate([a_bf16, a_bf16], axis=1)
    rhs = jnp.concatenate([bh, bl], axis=0)
    return jnp.dot(lhs, rhs, preferred_element_type=F32)


def _norm_matmul_body(x_ref, g_ref, w_ref, o_ref):
    xn = _rms(x_ref[...], g_ref[...])
    o_ref[...] = jnp.dot(xn.astype(BF16), w_ref[...], preferred_element_type=F32)


def _norm_matmul(x, g, w, tm):
    m, k = x.shape
    n = w.shape[1]
    return pl.pallas_call(
        _norm_matmul_body,
        grid=(m // tm,),
        in_specs=[pl.BlockSpec((tm, k), lambda i: (i, 0)),
                  pl.BlockSpec((1, k), lambda i: (0, 0)),
                  pl.BlockSpec((k, n), lambda i: (0, 0))],
        out_specs=pl.BlockSpec((tm, n), lambda i: (i, 0)),
        out_shape=jax.ShapeDtypeStruct((m, n), F32),
        compiler_params=_params(),
        name="norm_matmul",
    )(x, g, w)


def _sum_matmul_res_body(a_ref, w_ref, r_ref, o_ref):
    o_ref[...] = r_ref[...] + jnp.dot(a_ref[...].astype(BF16), w_ref[...], preferred_element_type=F32)


def _matmul_res(a, w, res, tm):
    m, k = a.shape
    n = w.shape[1]
    return pl.pallas_call(
        _sum_matmul_res_body,
        grid=(m // tm,),
        in_specs=[pl.BlockSpec((tm, k), lambda i: (i, 0)),
                  pl.BlockSpec((k, n), lambda i: (0, 0)),
                  pl.BlockSpec((tm, n), lambda i: (i, 0))],
        out_specs=pl.BlockSpec((tm, n), lambda i: (i, 0)),
        out_shape=jax.ShapeDtypeStruct((m, n), F32),
        compiler_params=_params(),
        name="matmul_res",
    )(a, w, res)


def _gmlp_body(u_ref, v_ref, ga_ref, vg_ref, ws_ref, bs_ref, ya_ref, vn_ref, *, rows):
    gv = _gelu(v_ref[...])
    vn = _rms(gv, vg_ref[...])
    vn_ref[...] = vn
    ri = lax.broadcasted_iota(jnp.int32, (rows, rows), 0)
    ci = lax.broadcasted_iota(jnp.int32, (rows, rows), 1)
    tril = ri >= ci
    gu = _gelu(u_ref[...])
    gate = _sigmoid(ga_ref[...])
    bias = bs_ref[...]
    width = vn.shape[1] // GMLP_GROUPS
    for g in range(GMLP_GROUPS):
        sl = slice(g * width, (g + 1) * width)
        w = jnp.where(tril, ws_ref[g], 0.0)
        sp = _bdot(w, vn[:, sl]) + bias[:, sl]
        ya_ref[:, sl] = gate[:, sl] * (gu[:, sl] * sp)


def _gmlp(uv, gates, vg, ws, bs_full, rows):
    m = uv.shape[0]
    w = uv.shape[1] // 2
    return pl.pallas_call(
        functools.partial(_gmlp_body, rows=rows),
        grid=(m // rows,),
        in_specs=[pl.BlockSpec((rows, w), lambda i: (i, 0)),
                  pl.BlockSpec((rows, w), lambda i: (i, 1)),
                  pl.BlockSpec((rows, w), lambda i: (i, 0)),
                  pl.BlockSpec((1, w), lambda i: (0, 0)),
                  pl.BlockSpec((GMLP_GROUPS, rows, rows), lambda i: (0, 0, 0)),
                  pl.BlockSpec((rows, w), lambda i: (0, 0))],
        out_specs=[pl.BlockSpec((rows, w), lambda i: (i, 0)),
                   pl.BlockSpec((rows, w), lambda i: (i, 0))],
        out_shape=[jax.ShapeDtypeStruct((m, w), F32), jax.ShapeDtypeStruct((m, w), F32)],
        compiler_params=_params(),
        name="gmlp",
    )(uv, uv, gates, vg, ws, bs_full)


def _rwkv_pair(r, kf, v, kkr, a, logd, st2, rk, lnw, lnb, consts):
    m0b, m1b, strict, incl, bd, bd_ones, tril_ones, n_levels = consts
    chunk = r.shape[0]

    def stack(z):
        zb = z.astype(BF16)
        return jnp.concatenate([zb * m0b, zb * m1b], axis=0)

    ss = _dot2_exact_rhs(kkr * kkr, bd_ones)
    cl = _dot2_exact_lhs(tril_ones, logd)
    bonus = _dot1(r * kf * rk, bd_ones) * v
    yield
    kk = kkr / jnp.maximum(jnp.sqrt(ss), 1e-12)
    cl_end = cl[chunk - 1:chunk, :]
    d_in = jnp.exp(cl)
    d_inv = jnp.exp(-cl)
    a_t = -(kk * jnp.exp(cl - logd))
    kb = kk * a
    b_t = kb * d_inv
    k_t = kf * d_inv
    r_t = r * d_in
    d_tail = jnp.exp(cl_end - cl)
    d_end = jnp.exp(cl_end)

    g = _dot1(jnp.concatenate([a_t, r_t], axis=0),
              jnp.concatenate([stack(b_t), stack(k_t)], axis=0), mode="nt")
    yield
    n_cat = jnp.where(strict, g[:chunk, :LANES], 0.0)
    a_ak = jnp.where(strict, g[:chunk, LANES:], 0.0)
    p_cat = jnp.where(incl, g[chunk:, :LANES], 0.0)
    q_cat = jnp.where(incl, g[chunk:, LANES:], 0.0)

    v_st = stack(v)
    st2b = st2.astype(BF16)
    x = _dot1(jnp.concatenate([a_t, a_ak], axis=1), jnp.concatenate([st2b, v_st], axis=0))
    yield
    n_pow = n_cat
    for lvl in range(n_levels):
        if lvl < n_levels - 1:
            z = _dot1(n_pow, jnp.concatenate([stack(x), stack(n_pow)], axis=1))
            yield
            x = x + z[:, :LANES]
            n_pow = z[:, LANES:]
        else:
            z = _dot1(n_pow, stack(x))
            yield
            x = x + z
    u = x

    o = _dot1(jnp.concatenate([r_t, p_cat, q_cat], axis=1),
              jnp.concatenate([st2b, stack(u), v_st], axis=0))
    st_add = _dot3(jnp.concatenate([kb * d_tail, kf * d_tail], axis=0), jnp.concatenate([u, v], axis=0), mode="tn")
    yield
    d_end_col = jnp.transpose(jnp.broadcast_to(d_end, (LANES, LANES)))
    st_new = st2 * d_end_col + jnp.where(bd, st_add, 0.0)

    inv_n = 1.0 / RWKV_HEAD_DIM
    mu = _dot1(o, bd_ones) * inv_n
    yield
    dlt = o - mu
    var = _dot1(dlt * dlt, bd_ones) * inv_n
    yield
    on = dlt * lax.rsqrt(var + GN_EPS) * lnw + lnb
    return on + bonus, st_new


def _interleave(generators):
    results = [None] * len(generators)
    live = list(range(len(generators)))
    while live:
        still = []
        for i in live:
            try:
                next(generators[i])
                still.append(i)
            except StopIteration as stop:
                results[i] = stop.value
        live = still
    return results


def _rwkv_body(rw_ref, gb_ref, ya_ref, sh0_ref, st0_ref, mu_ref, w0_ref, a0_ref, w2_ref, a2_ref, g2_ref,
               kk_ref, ka_ref, rk_ref, lnw_ref, lnb_ref, out_ref, stout_ref, last_ref, st_s, carry_s,
               *, chunk, t_valid, n_chunks):
    c = pl.program_id(1)
    n_seq, _, width = out_ref.shape
    n_pairs = width // LANES
    cols = rw_ref.shape[2]

    @pl.when(c == 0)
    def _init():
        st_s[...] = st0_ref[...]
        carry_s[...] = sh0_ref[...]

    def prepare(q):
        rw = rw_ref[q]
        row_w = lax.broadcasted_iota(jnp.int32, (chunk, cols), 0)
        rw_prev = jnp.where(row_w == 0, carry_s[q, 0:1, :], pltpu.roll(rw, 1, axis=0))
        last_row = rw[(t_valid - 1) % chunk:(t_valid - 1) % chunk + 1, :]
        carry_s[q, 0:1, :] = last_row
        last_ref[q] = jnp.broadcast_to(last_row, (8, cols))
        mix = rw + (rw_prev - rw) * mu_ref[...]

        r = mix[:, 0:width]
        k = mix[:, width:2 * width]
        v = mix[:, 2 * width:3 * width]
        wa = mix[:, 3 * width:3 * width + LANES]
        gl = mix[:, 3 * width + LANES:3 * width + 2 * LANES]
        logd = -DECAY_SCALE * _sigmoid(w0_ref[...] + _bdot(jnp.tanh(wa), w2_ref[...]))
        a = _sigmoid(a0_ref[...] + _bdot(wa, a2_ref[...]))
        gate_g = _bdot(_sigmoid(gl), g2_ref[...])
        kkr = k * kk_ref[...]
        kf = k * (1.0 + (a - 1.0) * ka_ref[...])
        if t_valid < chunk * n_chunks:
            valid = lax.broadcasted_iota(jnp.int32, (chunk, width), 0) + c * chunk < t_valid
            logd = jnp.where(valid, logd, 0.0)
            kkr = jnp.where(valid, kkr, 0.0)
            kf = jnp.where(valid, kf, 0.0)
            v = jnp.where(valid, v, 0.0)
        return r, kf, v, kkr, a, logd, gate_g

    lane = lax.broadcasted_iota(jnp.int32, (chunk, LANES), 1)
    rowi = lax.broadcasted_iota(jnp.int32, (chunk, LANES), 0)
    m0 = lane < RWKV_HEAD_DIM
    lane_in_head = jnp.where(m0, lane, lane - RWKV_HEAD_DIM)
    r2 = lax.broadcasted_iota(jnp.int32, (LANES, LANES), 0)
    c2 = lax.broadcasted_iota(jnp.int32, (LANES, LANES), 1)
    bd = (r2 < RWKV_HEAD_DIM) == (c2 < RWKV_HEAD_DIM)
    rl = lax.broadcasted_iota(jnp.int32, (chunk, chunk), 0)
    cl_ = lax.broadcasted_iota(jnp.int32, (chunk, chunk), 1)
    consts = (jnp.where(m0, 1.0, 0.0).astype(BF16), jnp.where(m0, 0.0, 1.0).astype(BF16),
              rowi > lane_in_head, rowi >= lane_in_head, bd,
              jnp.where(bd, 1.0, 0.0).astype(BF16), jnp.where(rl >= cl_, 1.0, 0.0).astype(BF16),
              chunk.bit_length() - 1)

    rk = rk_ref[...]
    lnw = lnw_ref[...]
    lnb = lnb_ref[...]
    gens, gate_gs = [], []
    for q in range(n_seq):
        r, kf, v, kkr, a, logd, gate_g = prepare(q)
        gate_gs.append(gate_g)
        for p in range(n_pairs):
            sl = slice(p * LANES, (p + 1) * LANES)
            gens.append(_rwkv_pair(r[:, sl], kf[:, sl], v[:, sl], kkr[:, sl], a[:, sl], logd[:, sl], st_s[q, p],
                                   rk[:, sl], lnw[:, sl], lnb[:, sl], consts))
    results = _interleave(gens)
    for q in range(n_seq):
        gate_b = _sigmoid(gb_ref[q])
        ya = ya_ref[q]
        for p in range(n_pairs):
            sl = slice(p * LANES, (p + 1) * LANES)
            y, st_new = results[q * n_pairs + p]
            st_s[q, p] = st_new
            out_ref[q, :, sl] = ya[:, sl] + gate_b[:, sl] * (y * gate_gs[q][:, sl])

    @pl.when(c == n_chunks - 1)
    def _fin():
        stout_ref[...] = st_s[...]


def _rwkv(rw, gates, ya, shift0, st0, p, t_valid):
    b, tp, cols = rw.shape
    width = ya.shape[2]
    n_pairs = width // LANES
    chunk = RWKV_CHUNK
    n_chunks = tp // chunk
    n_seq = RWKV_SEQS_PER_STEP
    assert b % n_seq == 0
    vec = lambda n: pl.BlockSpec((1, n), lambda i, j: (0, 0))
    lora = lambda n: pl.BlockSpec((n, width), lambda i, j: (0, 0))
    return pl.pallas_call(
        functools.partial(_rwkv_body, chunk=chunk, t_valid=t_valid, n_chunks=n_chunks),
        grid=(b // n_seq, n_chunks),
        in_specs=[pl.BlockSpec((n_seq, chunk, cols), lambda i, j: (i, j, 0)),
                  pl.BlockSpec((n_seq, chunk, width), lambda i, j: (i, j, 1)),
                  pl.BlockSpec((n_seq, chunk, width), lambda i, j: (i, j, 0)),
                  pl.BlockSpec((n_seq, 8, cols), lambda i, j: (i, 0, 0)),
                  pl.BlockSpec((n_seq, n_pairs, LANES, LANES), lambda i, j: (i, 0, 0, 0)),
                  vec(cols), vec(width), vec(width), lora(LANES), lora(LANES), lora(LANES),
                  vec(width), vec(width), vec(width), vec(width), vec(width)],
        out_specs=[pl.BlockSpec((n_seq, chunk, width), lambda i, j: (i, j, 0)),
                   pl.BlockSpec((n_seq, n_pairs, LANES, LANES), lambda i, j: (i, 0, 0, 0)),
                   pl.BlockSpec((n_seq, 8, cols), lambda i, j: (i, 0, 0))],
        out_shape=[jax.ShapeDtypeStruct((b, tp, width), F32),
                   jax.ShapeDtypeStruct((b, n_pairs, LANES, LANES), F32),
                   jax.ShapeDtypeStruct((b, 8, cols), F32)],
        scratch_shapes=[pltpu.VMEM((n_seq, n_pairs, LANES, LANES), F32), pltpu.VMEM((n_seq, 8, cols), F32)],
        compiler_params=_params(2),
        name="rwkv7",
    )(rw, gates, ya, shift0, st0, p["mu"], p["w0"], p["a0"], p["w2"], p["a2"], p["g2"],
      p["k_k"], p["k_a"], p["r_k"], p["lnx_w"], p["lnx_b"])


def _attn_body(x_ref, g_ref, wq_ref, mk_ref, mv_ref, wo_ref, o_ref):
    x = x_ref[0]
    q = jnp.dot(_rms(x, g_ref[...]).astype(BF16), wq_ref[...], preferred_element_type=F32)
    mk = mk_ref[0]
    mv = mv_ref[0]
    hd = x.shape[1] // CA_HEADS
    outs = []
    for h in range(CA_HEADS):
        sl = slice(h * hd, (h + 1) * hd)
        s = lax.dot_general(q[:, sl].astype(BF16), mk[:, sl], (((1,), (1,)), ((), ())),
                            preferred_element_type=F32) * (hd ** -0.5)
        e = jnp.exp(s - jnp.max(s, axis=-1, keepdims=True))
        pr = e / jnp.sum(e, axis=-1, keepdims=True)
        outs.append(jnp.dot(pr.astype(BF16), mv[:, sl], preferred_element_type=F32))
    o = jnp.concatenate(outs, axis=-1)
    o_ref[0] = x + jnp.dot(o.astype(BF16), wo_ref[...], preferred_element_type=F32)


def _cross_attn(x, g, wq, mk, mv, wo, tm):
    b, t, d = x.shape
    n_mem = mk.shape[1]
    return pl.pallas_call(
        _attn_body,
        grid=(b, t // tm),
        in_specs=[pl.BlockSpec((1, tm, d), lambda i, j: (i, j, 0)),
                  pl.BlockSpec((1, d), lambda i, j: (0, 0)),
                  pl.BlockSpec((d, d), lambda i, j: (0, 0)),
                  pl.BlockSpec((1, n_mem, d), lambda i, j: (i, 0, 0)),
                  pl.BlockSpec((1, n_mem, d), lambda i, j: (i, 0, 0)),
                  pl.BlockSpec((d, d), lambda i, j: (0, 0))],
        out_specs=pl.BlockSpec((1, tm, d), lambda i, j: (i, j, 0)),
        out_shape=jax.ShapeDtypeStruct((b, t, d), F32),
        compiler_params=_params(2),
        name="cross_attn",
    )(x, g, wq, mk, mv, wo)


def _store_token_tiles(ref, rows):
    n, d = rows.shape
    sub = d // LANES
    for s in range(sub):
        ref[pl.ds(s, n, stride=sub), :] = rows[:, s * LANES:(s + 1) * LANES]


def _load_token_tiles(ref, sub):
    n = ref.shape[0] // sub
    return jnp.concatenate([ref[pl.ds(s, n, stride=sub), :] for s in range(sub)], axis=1)


def _router_body(x_ref, g_ref, wr_ref, br_ref, h_ref, gate_ref, idx_ref):
    h = _rms(x_ref[...], g_ref[...])
    _store_token_tiles(h_ref, h)
    logits = _dot3(h, wr_ref[...]) + br_ref[...]
    lane = lax.broadcasted_iota(jnp.int32, logits.shape, 1)
    neg = jnp.float32(-jnp.inf)
    vals, idxs = [], []
    for _ in range(TOP_K):
        m = jnp.max(logits, axis=-1, keepdims=True)
        idx = jnp.min(jnp.where(logits == m, lane, LANES), axis=-1, keepdims=True)
        vals.append(m)
        idxs.append(idx)
        logits = jnp.where(lane == idx, neg, logits)
    es = [jnp.exp(vv - vals[0]) for vv in vals]
    den = es[0] + es[1] + es[2] + es[3]
    gate = jnp.zeros(logits.shape, F32)
    sel = jnp.zeros(logits.shape, jnp.int32)
    for j in range(TOP_K):
        gate = jnp.where(lane == j, es[j] / den, gate)
        sel = jnp.where(lane == j, idxs[j], sel)
    gate_ref[...] = gate
    idx_ref[...] = sel


def _router(x, g, wr_pad, br_pad, tm):
    m, d = x.shape
    return pl.pallas_call(
        _router_body,
        grid=(m // tm,),
        in_specs=[pl.BlockSpec((tm, d), lambda i: (i, 0)),
                  pl.BlockSpec((1, d), lambda i: (0, 0)),
                  pl.BlockSpec((d, LANES), lambda i: (0, 0)),
                  pl.BlockSpec((1, LANES), lambda i: (0, 0))],
        out_specs=[pl.BlockSpec((tm * d // LANES, LANES), lambda i: (i, 0)),
                   pl.BlockSpec((tm, LANES), lambda i: (i, 0)),
                   pl.BlockSpec((tm, LANES), lambda i: (i, 0))],
        out_shape=[jax.ShapeDtypeStruct((m * d // LANES, LANES), F32),
                   jax.ShapeDtypeStruct((m, LANES), F32),
                   jax.ShapeDtypeStruct((m, LANES), jnp.int32)],
        compiler_params=_params(),
        name="moe_router",
    )(x, g, wr_pad, br_pad)


def _wait_rows(make_copy, n_rows):
    for bit in reversed(range(MOE_BLOCK.bit_length())):
        size = 1 << bit

        @pl.when((n_rows & size) != 0)
        def _wait(size=size):
            make_copy(size).wait()


def _expert_body(blk_e_ref, n_valid_ref, n_used_ref, slot_ref, slot_next_ref, h_ref, wgu_ref, bgu_ref,
                 wdn_ref, bdn_ref, y_ref, xbuf, ybuf, gsem, ssem, *, n_blk):
    del blk_e_ref
    i = pl.program_id(0)
    n_used = n_used_ref[0]
    cur = lax.rem(i, 2)
    f = wdn_ref.shape[1]
    sub = wdn_ref.shape[2] // LANES
    m = h_ref.shape[0] // sub

    def token_of(slot):
        return (slot & (m - 1)) if m & (m - 1) == 0 else lax.rem(slot, m)

    def tiles(first, count=1):
        start = first * sub if isinstance(first, int) else pl.multiple_of(first * sub, sub)
        return pl.ds(start, count * sub)

    def for_rows(n_rows, row_fn):
        @pl.when(n_rows == MOE_BLOCK)
        def _full():
            def body(g, carry):
                for u in range(ROW_UNROLL):
                    row_fn(g * ROW_UNROLL + u, u % 2)
                return carry
            lax.fori_loop(0, MOE_BLOCK // ROW_UNROLL, body, 0)

        @pl.when(n_rows != MOE_BLOCK)
        def _partial():
            def body(r, carry):
                row_fn(r, 0)
                return carry
            lax.fori_loop(0, n_rows, body, 0)

    def gather_start(slots, n_rows, b):
        def row(r, priority):
            pltpu.make_async_copy(h_ref.at[tiles(token_of(slots[0, 0, r]))], xbuf.at[b, tiles(r)],
                                  gsem.at[b]).start(priority=priority)
        for_rows(n_rows, row)

    def gather_wait(n_rows, b):
        _wait_rows(lambda k: pltpu.make_async_copy(h_ref.at[tiles(0, k)], xbuf.at[b, tiles(0, k)], gsem.at[b]),
                   n_rows)

    def scatter_start(slots, n_rows, b):
        def row(r, priority):
            pltpu.make_async_copy(ybuf.at[b, tiles(r)], y_ref.at[tiles(slots[0, 0, r])],
                                  ssem.at[b]).start(priority=priority)
        for_rows(n_rows, row)

    def scatter_wait(n_rows, b):
        _wait_rows(lambda k: pltpu.make_async_copy(ybuf.at[b, tiles(0, k)], y_ref.at[tiles(0, k)], ssem.at[b]),
                   n_rows)

    @pl.when(i == 0)
    def _first():
        xbuf[...] = jnp.zeros(xbuf.shape, F32)
        gather_start(slot_ref, n_valid_ref[0], 0)

    @pl.when(i + 1 < n_used)
    def _prefetch():
        gather_start(slot_next_ref, n_valid_ref[i + 1], 1 - cur)

    @pl.when(i < n_used)
    def _run():
        gather_wait(n_valid_ref[i], cur)

        @pl.when(i >= 2)
        def _free_ybuf():
            scatter_wait(n_valid_ref[i - 2], cur)

        x = _load_token_tiles(xbuf.at[cur], sub)
        hh = jnp.dot(x.astype(BF16), wgu_ref[0], preferred_element_type=F32) + bgu_ref[0]
        hg = jnp.minimum(hh[:, :f], SWIGLU_LIMIT)
        hl = jnp.clip(hh[:, f:], -SWIGLU_LIMIT, SWIGLU_LIMIT)
        act = hg * _sigmoid(SWIGLU_ALPHA * hg) * (hl + 1.0)
        _store_token_tiles(ybuf.at[cur], jnp.dot(act.astype(BF16), wdn_ref[0], preferred_element_type=F32)
                           + bdn_ref[0])
        scatter_start(slot_ref, n_valid_ref[i], cur)

    @pl.when(i == n_blk - 1)
    def _drain():
        last = n_used - 1

        @pl.when(last >= 1)
        def _prev():
            scatter_wait(n_valid_ref[last - 1], lax.rem(last - 1, 2))

        @pl.when(last >= 0)
        def _last():
            scatter_wait(n_valid_ref[last], lax.rem(last, 2))


def _experts(h, slots, blk_e, n_valid, n_used, wgu, bgu, wdn, bdn):
    f, d = wdn.shape[1:]
    n_sub = d // LANES
    n_blk = slots.shape[0]
    grid_spec = pltpu.PrefetchScalarGridSpec(
        num_scalar_prefetch=3,
        grid=(n_blk,),
        in_specs=[pl.BlockSpec((1, 1, MOE_BLOCK), lambda i, be, nv, nu: (i, 0, 0), memory_space=pltpu.SMEM),
                  pl.BlockSpec((1, 1, MOE_BLOCK), lambda i, be, nv, nu: (jnp.minimum(i + 1, n_blk - 1), 0, 0),
                               memory_space=pltpu.SMEM),
                  pl.BlockSpec(memory_space=pl.ANY),
                  pl.BlockSpec((1, d, 2 * f), lambda i, be, nv, nu: (be[i], 0, 0)),
                  pl.BlockSpec((1, 1, 2 * f), lambda i, be, nv, nu: (be[i], 0, 0)),
                  pl.BlockSpec((1, f, d), lambda i, be, nv, nu: (be[i], 0, 0)),
                  pl.BlockSpec((1, 1, d), lambda i, be, nv, nu: (be[i], 0, 0))],
        out_specs=pl.BlockSpec(memory_space=pl.ANY),
        scratch_shapes=[pltpu.VMEM((2, MOE_BLOCK * n_sub, LANES), F32), pltpu.VMEM((2, MOE_BLOCK * n_sub, LANES), F32),
                        pltpu.SemaphoreType.DMA((2,)), pltpu.SemaphoreType.DMA((2,))],
    )
    return pl.pallas_call(
        functools.partial(_expert_body, n_blk=n_blk),
        grid_spec=grid_spec,
        out_shape=jax.ShapeDtypeStruct((h.shape[0] * TOP_K, LANES), F32),
        compiler_params=_params(),
        name="moe_experts",
    )(blk_e, n_valid, n_used, slots, slots, h, wgu, bgu, wdn, bdn)


def _combine_body(gate_ref, x_ref, fg_ref, *refs):
    y_refs, o_ref = refs[:TOP_K], refs[TOP_K]
    acc = x_ref[...]
    gate = gate_ref[...]
    sub = acc.shape[1] // LANES
    for k in range(TOP_K):
        acc = acc + gate[:, k:k + 1] * _load_token_tiles(y_refs[k], sub)
    o_ref[...] = _rms(acc, fg_ref[...])


def _combine(gate, x, fg, y, tm):
    m, d = x.shape
    n_sub = d // LANES
    n_tiles = m // tm
    y_spec = lambda k: pl.BlockSpec((tm * n_sub, LANES), lambda i: (k * n_tiles + i, 0))
    return pl.pallas_call(
        _combine_body,
        grid=(n_tiles,),
        in_specs=[pl.BlockSpec((tm, LANES), lambda i: (i, 0)),
                  pl.BlockSpec((tm, d), lambda i: (i, 0)),
                  pl.BlockSpec((1, d), lambda i: (0, 0))] + [y_spec(k) for k in range(TOP_K)],
        out_specs=pl.BlockSpec((tm, d), lambda i: (i, 0)),
        out_shape=jax.ShapeDtypeStruct((m, d), F32),
        compiler_params=_params(),
        name="moe_combine",
    )(gate, x, fg, *([y] * TOP_K))


def _moe_final(x2d, w, tm):
    m, d = x2d.shape
    h, gate, sel = _router(x2d, w["norm_ffn_g"], w["router_w"], w["router_b"], tm)
    flat_e = sel[:, :TOP_K].T.reshape(-1)
    n_assign = m * TOP_K
    order = jnp.argsort(flat_e, stable=True).astype(jnp.int32)
    experts = jnp.arange(N_EXPERTS, dtype=jnp.int32)
    counts = jnp.sum((flat_e[:, None] == experts[None, :]).astype(jnp.int32), axis=0)
    padded = (counts + MOE_BLOCK - 1) // MOE_BLOCK * MOE_BLOCK
    pad_end = jnp.cumsum(padded)
    pad_start = pad_end - padded
    start = jnp.cumsum(counts) - counts
    n_blk = -(-(n_assign + N_EXPERTS * (MOE_BLOCK - 1)) // MOE_BLOCK)
    blk_row0 = jnp.arange(n_blk, dtype=jnp.int32) * MOE_BLOCK
    blk_e = jnp.minimum(jnp.sum((blk_row0[:, None] >= pad_end[None, :]).astype(jnp.int32), axis=1), N_EXPERTS - 1)
    n_used = (pad_end[-1:] // MOE_BLOCK).astype(jnp.int32)
    off = blk_row0 - pad_start[blk_e]
    n_valid = jnp.clip(counts[blk_e] - off, 0, MOE_BLOCK).astype(jnp.int32)
    within = jnp.arange(MOE_BLOCK, dtype=jnp.int32)[None, :]
    pos = jnp.clip((start[blk_e] + off)[:, None] + within, 0, n_assign - 1)
    slots = jnp.where(within < n_valid[:, None], order[pos], 0).astype(jnp.int32).reshape(n_blk, 1, MOE_BLOCK)
    y = _experts(h, slots, blk_e, n_valid, n_used, w["moe_w_gu"], w["moe_b_gu"], w["moe_w_down"], w["moe_b_down"])
    return _combine(gate, x2d, w["final_norm_g"], y, tm)


def _layer(x, shift_prev, s0, mk, mv, w):
    b, t, d = x.shape
    m = b * t
    tm = min(256, m)
    x2d = x.reshape(m, d)
    uv = _norm_matmul(x2d, w["norm_mix_g"], w["w_uv"], tm)
    rw = _norm_matmul(x2d, w["norm_mix_g"], w["w_rw"], tm)
    gates = _norm_matmul(x2d, w["norm_mix_g"], w["w_gates"], tm)
    cols = rw.shape[1]

    rows = min(t, GMLP_CHUNK)
    ws = w["gmlp_ws"][:, :rows, :rows]
    bs_full = jnp.repeat(w["gmlp_bs"][:, :rows].T, d // GMLP_GROUPS, axis=1)
    ya, v_n = _gmlp(uv, gates, w["gmlp_v_norm_g"], ws, bs_full, rows)

    n_heads = s0.shape[1]
    n_pairs = n_heads // 2
    tp = -(-t // RWKV_CHUNK) * RWKV_CHUNK
    pad3 = lambda z: jnp.pad(z.reshape(b, t, -1), ((0, 0), (0, tp - t), (0, 0)))
    st_t = jnp.swapaxes(s0, 2, 3).reshape(b, n_pairs, 2, RWKV_HEAD_DIM, RWKV_HEAD_DIM)
    zero = jnp.zeros_like(st_t[:, :, 0])
    st0 = jnp.concatenate([jnp.concatenate([st_t[:, :, 0], zero], axis=-1),
                           jnp.concatenate([zero, st_t[:, :, 1]], axis=-1)], axis=-2)
    shift0 = jnp.broadcast_to(shift_prev, (b, 8, cols))
    merged, st_out, last = _rwkv(pad3(rw), pad3(gates), pad3(ya), shift0, st0, w, t)
    merged = merged[:, :t].reshape(m, d)
    s_new = jnp.stack([st_out[:, :, :RWKV_HEAD_DIM, :RWKV_HEAD_DIM], st_out[:, :, RWKV_HEAD_DIM:, RWKV_HEAD_DIM:]],
                      axis=2).reshape(b, n_heads, RWKV_HEAD_DIM, RWKV_HEAD_DIM)
    s_new = jnp.swapaxes(s_new, 2, 3)
    shift_new = last[:, :1, :]

    x1 = _matmul_res(merged, w["w_out"], x2d, tm)
    x2 = _cross_attn(x1.reshape(b, t, d), w["norm_ca_g"], w["ca_wq"], mk, mv, w["ca_wo"], min(512, t))
    y = _moe_final(x2.reshape(m, d), w, min(256, m))
    return y.reshape(b, t, d), shift_new, s_new, v_n.reshape(b, t, d)


def kernel(x_prompt, x_sample, mem_prompt, state_shift, state_rwkv, cache_mem_k, cache_mem_v, norm_mix_g, w_in, gmlp_v_norm_g, gmlp_ws, gmlp_bs, rwkv_mu, rwkv_w0, rwkv_w2, rwkv_a0, rwkv_a2, rwkv_g2, rwkv_k_k, rwkv_k_a, rwkv_r_k, rwkv_lnx_w, rwkv_lnx_b, w_out, norm_ca_g, norm_mem_g, ca_wq, ca_wk, ca_wv, ca_wo, norm_ffn_g, router_w, router_b, moe_w_gu, moe_b_gu, moe_w_down, moe_b_down, final_norm_g):
    depth = w_in.shape[0]
    assert depth == 1, "the fused MoE + final-norm tail assumes a single layer"
    l = 0
    bp, _, d = x_prompt.shape
    width = rwkv_w0.shape[1]
    n_heads = width // RWKV_HEAD_DIM
    gw = gmlp_v_norm_g.shape[1]
    row = lambda z: z.reshape(1, -1)
    zpad = jnp.zeros((LANES - rwkv_w2.shape[1], width), F32)
    w = dict(
        norm_mix_g=row(norm_mix_g[l]),
        w_uv=w_in[l][:, :2 * gw].astype(BF16),
        w_rw=w_in[l][:, 2 * gw:w_in.shape[2] - 2 * d].astype(BF16),
        w_gates=w_in[l][:, w_in.shape[2] - 2 * d:].astype(BF16),
        gmlp_v_norm_g=row(gmlp_v_norm_g[l]), gmlp_ws=gmlp_ws[l], gmlp_bs=gmlp_bs[l],
        mu=row(rwkv_mu[l]), w0=row(rwkv_w0[l]), a0=row(rwkv_a0[l]),
        w2=jnp.concatenate([rwkv_w2[l], zpad], axis=0).astype(BF16),
        a2=jnp.concatenate([zpad, rwkv_a2[l]], axis=0).astype(BF16),
        g2=rwkv_g2[l].astype(BF16),
        k_k=row(rwkv_k_k[l]), k_a=row(rwkv_k_a[l]), r_k=row(rwkv_r_k[l]),
        lnx_w=row(rwkv_lnx_w[l]), lnx_b=row(rwkv_lnx_b[l]),
        w_out=w_out[l].astype(BF16), norm_ca_g=row(norm_ca_g[l]),
        ca_wq=ca_wq[l].astype(BF16), ca_wo=ca_wo[l].astype(BF16),
        norm_ffn_g=row(norm_ffn_g[l]),
        router_w=jnp.pad(router_w[l], ((0, 0), (0, LANES - N_EXPERTS))),
        router_b=jnp.pad(row(router_b[l]), ((0, 0), (0, LANES - N_EXPERTS)), constant_values=-jnp.inf),
        moe_w_gu=moe_w_gu[l].astype(BF16), moe_b_gu=moe_b_gu[l][:, None, :],
        moe_w_down=moe_w_down[l].astype(BF16), moe_b_down=moe_b_down[l][:, None, :],
        final_norm_g=row(final_norm_g),
    )

    n_mem = mem_prompt.shape[1]
    mem2d = mem_prompt.reshape(bp * n_mem, d)
    mk = _norm_matmul(mem2d, row(norm_mem_g[l]), ca_wk[l].astype(BF16), 256).reshape(bp, n_mem, d)
    mv = _norm_matmul(mem2d, row(norm_mem_g[l]), ca_wv[l].astype(BF16), 256).reshape(bp, n_mem, d)
    shift0 = jnp.zeros((bp, 1, rwkv_mu.shape[1]), F32)
    s_zero = jnp.zeros((bp, n_heads, RWKV_HEAD_DIM, RWKV_HEAD_DIM), F32)
    y_p, sh_p, st_p, _ = _layer(x_prompt, shift0, s_zero, mk.astype(BF16), mv.astype(BF16), w)

    bs = x_sample.shape[0]
    cmk = cache_mem_k[l].reshape(bs, n_mem, d).astype(BF16)
    cmv = cache_mem_v[l].reshape(bs, n_mem, d).astype(BF16)
    y_s, sh_s, st_s, v_s = _layer(x_sample, state_shift[l], state_rwkv[l], cmk, cmv, w)

    ca_shape = (1, bp, n_mem, CA_HEADS, d // CA_HEADS)
    return (y_p, y_s, sh_p[None], st_p[None], mk.reshape(ca_shape), mv.reshape(ca_shape),
            sh_s[None], st_s[None], v_s[None])
```

```python
import functools

import jax
import jax.numpy as jnp
from jax import lax
from jax.experimental import pallas as pl
from jax.experimental.pallas import tpu as pltpu

F32 = jnp.float32
BF16 = jnp.bfloat16

NORM_EPS = 1e-5
GN_EPS = 64e-5
RWKV_HEAD_DIM = 64
RWKV_CHUNK = 64
RWKV_SEQS_PER_STEP = 2
DECAY_SCALE = 0.6065306597126334
GMLP_CHUNK = 128
GMLP_GROUPS = 8
CA_HEADS = 4
N_EXPERTS = 32
TOP_K = 4
SWIGLU_LIMIT = 7.0
SWIGLU_ALPHA = 1.702
LANES = 128
MOE_BLOCK = 256
ROW_UNROLL = 8
VMEM_LIMIT = 56 * 1024 * 1024


def _params(n_axes=1):
    return pltpu.CompilerParams(dimension_semantics=("arbitrary",) * n_axes, vmem_limit_bytes=VMEM_LIMIT)


def _rms(x, g):
    return x * lax.rsqrt(jnp.mean(x * x, axis=-1, keepdims=True) + NORM_EPS) * g


def _gelu(x):
    return x * (lax.erf(x * 0.7071067811865476) + 1.0) * 0.5


def _sigmoid(x):
    return 1.0 / (1.0 + jnp.exp(-x))


def _bdot(a, b):
    return jnp.dot(a.astype(BF16), b.astype(BF16), preferred_element_type=F32)


_DOT_DIMS = {"nn": (((1,), (0,)), ((), ())), "nt": (((1,), (1,)), ((), ())), "tn": (((0,), (0,)), ((), ()))}


def _dot1(a, b, mode="nn"):
    return lax.dot_general(a.astype(BF16), b.astype(BF16), _DOT_DIMS[mode], preferred_element_type=F32)


def _hi_lo(x):
    hi = x.astype(BF16)
    lo = (x - hi.astype(F32)).astype(BF16)
    return hi, lo


def _dot3(a, b, mode="nn"):
    ah, al = _hi_lo(a)
    bh, bl = _hi_lo(b)
    if mode == "nn":
        lhs = jnp.concatenate([ah, ah, al], axis=1)
        rhs = jnp.concatenate([bh, bl, bh], axis=0)
        dims = (((1,), (0,)), ((), ()))
    elif mode == "nt":
        lhs = jnp.concatenate([ah, ah, al], axis=1)
        rhs = jnp.concatenate([bh, bl, bh], axis=1)
        dims = (((1,), (1,)), ((), ()))
    else:
        lhs = jnp.concatenate([ah, ah, al], axis=0)
        rhs = jnp.concatenate([bh, bl, bh], axis=0)
        dims = (((0,), (0,)), ((), ()))
    return lax.dot_general(lhs, rhs, dims, preferred_element_type=F32)


def _dot2_exact_rhs(a, b_bf16):
    ah, al = _hi_lo(a)
    lhs = jnp.concatenate([ah, al], axis=1)
    rhs = jnp.concatenate([b_bf16, b_bf16], axis=0)
    return jnp.dot(lhs, rhs, preferred_element_type=F32)


def _dot2_exact_lhs(a_bf16, b):
    bh, bl = _hi_lo(b)
    lhs = jnp.concatenate([a_bf16, a_bf16], axis=1)
    rhs = jnp.concatenate([bh, bl], axis=0)
    return jnp.dot(lhs, rhs, preferred_element_type=F32)


def _norm_matmul_body(x_ref, g_ref, w_ref, o_ref):
    xn = _rms(x_ref[...], g_ref[...])
    o_ref[...] = jnp.dot(xn.astype(BF16), w_ref[...], preferred_element_type=F32)


def _norm_matmul(x, g, w, tm):
    m, k = x.shape
    n = w.shape[1]
    return pl.pallas_call(
        _norm_matmul_body,
        grid=(m // tm,),
        in_specs=[pl.BlockSpec((tm, k), lambda i: (i, 0)),
                  pl.BlockSpec((1, k), lambda i: (0, 0)),
                  pl.BlockSpec((k, n), lambda i: (0, 0))],
        out_specs=pl.BlockSpec((tm, n), lambda i: (i, 0)),
        out_shape=jax.ShapeDtypeStruct((m, n), F32),
        compiler_params=_params(),
        name="norm_matmul",
    )(x, g, w)


def _mixer_in_body(x_ref, g_ref, wa_ref, wb_ref, vg_ref, ws_ref, bs_ref, ya_ref, rw_ref, gb_ref, *vn_refs, rows):
    xn = _rms(x_ref[...], g_ref[...]).astype(BF16)
    pb = jnp.dot(xn, wb_ref[...], preferred_element_type=F32)
    n_rw = rw_ref.shape[1]
    rw_ref[...] = pb[:, :n_rw]
    gb_ref[...] = pb[:, n_rw:]
    pa = jnp.dot(xn, wa_ref[...], preferred_element_type=F32)
    tm = pa.shape[0]
    wid = ya_ref.shape[1]
    gu = _gelu(pa[:, :wid])
    vn = _rms(_gelu(pa[:, wid:2 * wid]), vg_ref[...])
    gate = _sigmoid(pa[:, 2 * wid:])
    if vn_refs:
        vn_refs[0][...] = vn
    ri = lax.broadcasted_iota(jnp.int32, (rows, rows), 0)
    ci = lax.broadcasted_iota(jnp.int32, (rows, rows), 1)
    tril = ri >= ci
    bias = bs_ref[...]
    gw = wid // GMLP_GROUPS
    for g in range(GMLP_GROUPS):
        sl = slice(g * gw, (g + 1) * gw)
        w = jnp.where(tril, ws_ref[g], 0.0).astype(BF16)
        for c in range(tm // rows):
            rs = slice(c * rows, (c + 1) * rows)
            sp = jnp.dot(w, vn[rs, sl].astype(BF16), preferred_element_type=F32) + bias[:, sl]
            ya_ref[rs, sl] = gate[rs, sl] * (gu[rs, sl] * sp)


def _mixer_in(x, g, wa, wb, vg, ws, bs_full, rows, tm, emit_vn):
    m, d = x.shape
    wid = wa.shape[1] // 3
    n_rw = wb.shape[1] - d
    resident = lambda shape: pl.BlockSpec(shape, lambda i: (0,) * len(shape), pipeline_mode=pl.Buffered(1))
    row_tile = lambda n: pl.BlockSpec((tm, n), lambda i: (i, 0))
    out_widths = [wid, n_rw, d] + ([wid] if emit_vn else [])
    return pl.pallas_call(
        functools.partial(_mixer_in_body, rows=rows),
        grid=(m // tm,),
        in_specs=[row_tile(d), resident((1, d)), resident(wa.shape), resident(wb.shape), resident((1, wid)),
                  resident(ws.shape), resident(bs_full.shape)],
        out_specs=[row_tile(n) for n in out_widths],
        out_shape=[jax.ShapeDtypeStruct((m, n), F32) for n in out_widths],
        compiler_params=_params(),
        name="mixer_in",
    )(x, g, wa, wb, vg, ws, bs_full)


def _rwkv_pair(r, kf, v, kkr, a, logd, st2, rk, lnw, lnb, consts):
    m0b, m1b, strict, incl, bd, bd_ones, tril_ones, n_levels = consts
    chunk = r.shape[0]

    def stack(z):
        zb = z.astype(BF16)
        return jnp.concatenate([zb * m0b, zb * m1b], axis=0)

    ss = _dot2_exact_rhs(kkr * kkr, bd_ones)
    cl = _dot2_exact_lhs(tril_ones, logd)
    bonus = _dot1(r * kf * rk, bd_ones) * v
    yield
    kk = kkr / jnp.maximum(jnp.sqrt(ss), 1e-12)
    cl_end = cl[chunk - 1:chunk, :]
    d_in = jnp.exp(cl)
    d_inv = jnp.exp(-cl)
    a_t = -(kk * jnp.exp(cl - logd))
    kb = kk * a
    b_t = kb * d_inv
    k_t = kf * d_inv
    r_t = r * d_in
    d_tail = jnp.exp(cl_end - cl)
    d_end = jnp.exp(cl_end)

    g = _dot1(jnp.concatenate([a_t, r_t], axis=0),
              jnp.concatenate([stack(b_t), stack(k_t)], axis=0), mode="nt")
    yield
    n_cat = jnp.where(strict, g[:chunk, :LANES], 0.0)
    a_ak = jnp.where(strict, g[:chunk, LANES:], 0.0)
    p_cat = jnp.where(incl, g[chunk:, :LANES], 0.0)
    q_cat = jnp.where(incl, g[chunk:, LANES:], 0.0)

    v_st = stack(v)
    st2b = st2.astype(BF16)
    x = _dot1(jnp.concatenate([a_t, a_ak], axis=1), jnp.concatenate([st2b, v_st], axis=0))
    yield
    n_pow = n_cat
    for lvl in range(n_levels):
        if lvl < n_levels - 1:
            z = _dot1(n_pow, jnp.concatenate([stack(x), stack(n_pow)], axis=1))
            yield
            x = x + z[:, :LANES]
            n_pow = z[:, LANES:]
        else:
            z = _dot1(n_pow, stack(x))
            yield
            x = x + z
    u = x

    o = _dot1(jnp.concatenate([r_t, p_cat, q_cat], axis=1),
              jnp.concatenate([st2b, stack(u), v_st], axis=0))
    st_add = _dot3(jnp.concatenate([kb * d_tail, kf * d_tail], axis=0), jnp.concatenate([u, v], axis=0), mode="tn")
    yield
    d_end_col = jnp.transpose(jnp.broadcast_to(d_end, (LANES, LANES)))
    st_new = st2 * d_end_col + jnp.where(bd, st_add, 0.0)

    inv_n = 1.0 / RWKV_HEAD_DIM
    mu = _dot1(o, bd_ones) * inv_n
    yield
    dlt = o - mu
    var = _dot1(dlt * dlt, bd_ones) * inv_n
    yield
    on = dlt * lax.rsqrt(var + GN_EPS) * lnw + lnb
    return on + bonus, st_new


def _interleave(generators):
    results = [None] * len(generators)
    live = list(range(len(generators)))
    while live:
        still = []
        for i in live:
            try:
                next(generators[i])
                still.append(i)
            except StopIteration as stop:
                results[i] = stop.value
        live = still
    return results


def _rwkv_body(rw_ref, gb_ref, ya_ref, sh0_ref, st0_ref, mu_ref, w0_ref, a0_ref, w2_ref, a2_ref, g2_ref,
               kk_ref, ka_ref, rk_ref, lnw_ref, lnb_ref, out_ref, stout_ref, last_ref, st_s, carry_s,
               *, chunk, t_valid, n_chunks):
    c = pl.program_id(1)
    n_seq, _, width = out_ref.shape
    n_pairs = width // LANES
    cols = rw_ref.shape[2]

    @pl.when(c == 0)
    def _init():
        st_s[...] = st0_ref[...]
        carry_s[...] = sh0_ref[...]

    def prepare(q):
        rw = rw_ref[q]
        row_w = lax.broadcasted_iota(jnp.int32, (chunk, cols), 0)
        rw_prev = jnp.where(row_w == 0, carry_s[q, 0:1, :], pltpu.roll(rw, 1, axis=0))
        last_row = rw[(t_valid - 1) % chunk:(t_valid - 1) % chunk + 1, :]
        carry_s[q, 0:1, :] = last_row
        last_ref[q] = jnp.broadcast_to(last_row, (8, cols))
        mix = rw + (rw_prev - rw) * mu_ref[...]

        r = mix[:, 0:width]
        k = mix[:, width:2 * width]
        v = mix[:, 2 * width:3 * width]
        wa = mix[:, 3 * width:3 * width + LANES]
        gl = mix[:, 3 * width + LANES:3 * width + 2 * LANES]
        logd = -DECAY_SCALE * _sigmoid(w0_ref[...] + _bdot(jnp.tanh(wa), w2_ref[...]))
        a = _sigmoid(a0_ref[...] + _bdot(wa, a2_ref[...]))
        gate_g = _bdot(_sigmoid(gl), g2_ref[...])
        kkr = k * kk_ref[...]
        kf = k * (1.0 + (a - 1.0) * ka_ref[...])
        if t_valid < chunk * n_chunks:
            valid = lax.broadcasted_iota(jnp.int32, (chunk, width), 0) + c * chunk < t_valid
            logd = jnp.where(valid, logd, 0.0)
            kkr = jnp.where(valid, kkr, 0.0)
            kf = jnp.where(valid, kf, 0.0)
            v = jnp.where(valid, v, 0.0)
        return r, kf, v, kkr, a, logd, gate_g

    lane = lax.broadcasted_iota(jnp.int32, (chunk, LANES), 1)
    rowi = lax.broadcasted_iota(jnp.int32, (chunk, LANES), 0)
    m0 = lane < RWKV_HEAD_DIM
    lane_in_head = jnp.where(m0, lane, lane - RWKV_HEAD_DIM)
    r2 = lax.broadcasted_iota(jnp.int32, (LANES, LANES), 0)
    c2 = lax.broadcasted_iota(jnp.int32, (LANES, LANES), 1)
    bd = (r2 < RWKV_HEAD_DIM) == (c2 < RWKV_HEAD_DIM)
    rl = lax.broadcasted_iota(jnp.int32, (chunk, chunk), 0)
    cl_ = lax.broadcasted_iota(jnp.int32, (chunk, chunk), 1)
    consts = (jnp.where(m0, 1.0, 0.0).astype(BF16), jnp.where(m0, 0.0, 1.0).astype(BF16),
              rowi > lane_in_head, rowi >= lane_in_head, bd,
              jnp.where(bd, 1.0, 0.0).astype(BF16), jnp.where(rl >= cl_, 1.0, 0.0).astype(BF16),
              chunk.bit_length() - 1)

    rk = rk_ref[...]
    lnw = lnw_ref[...]
    lnb = lnb_ref[...]
    gens, gate_gs = [], []
    for q in range(n_seq):
        r, kf, v, kkr, a, logd, gate_g = prepare(q)
        gate_gs.append(gate_g)
        for p in range(n_pairs):
            sl = slice(p * LANES, (p + 1) * LANES)
            gens.append(_rwkv_pair(r[:, sl], kf[:, sl], v[:, sl], kkr[:, sl], a[:, sl], logd[:, sl], st_s[q, p],
                                   rk[:, sl], lnw[:, sl], lnb[:, sl], consts))
    results = _interleave(gens)
    for q in range(n_seq):
        gate_b = _sigmoid(gb_ref[q])
        ya = ya_ref[q]
        for p in range(n_pairs):
            sl = slice(p * LANES, (p + 1) * LANES)
            y, st_new = results[q * n_pairs + p]
            st_s[q, p] = st_new
            out_ref[q, :, sl] = ya[:, sl] + gate_b[:, sl] * (y * gate_gs[q][:, sl])

    @pl.when(c == n_chunks - 1)
    def _fin():
        stout_ref[...] = st_s[...]


def _rwkv(rw, gates, ya, shift0, st0, p, t_valid):
    b, tp, cols = rw.shape
    width = ya.shape[2]
    n_pairs = width // LANES
    chunk = RWKV_CHUNK
    n_chunks = tp // chunk
    n_seq = RWKV_SEQS_PER_STEP
    assert b % n_seq == 0
    vec = lambda n: pl.BlockSpec((1, n), lambda i, j: (0, 0))
    lora = lambda n: pl.BlockSpec((n, width), lambda i, j: (0, 0))
    return pl.pallas_call(
        functools.partial(_rwkv_body, chunk=chunk, t_valid=t_valid, n_chunks=n_chunks),
        grid=(b // n_seq, n_chunks),
        in_specs=[pl.BlockSpec((n_seq, chunk, cols), lambda i, j: (i, j, 0)),
                  pl.BlockSpec((n_seq, chunk, width), lambda i, j: (i, j, 0)),
                  pl.BlockSpec((n_seq, chunk, width), lambda i, j: (i, j, 0)),
                  pl.BlockSpec((n_seq, 8, cols), lambda i, j: (i, 0, 0)),
                  pl.BlockSpec((n_seq, n_pairs, LANES, LANES), lambda i, j: (i, 0, 0, 0)),
                  vec(cols), vec(width), vec(width), lora(LANES), lora(LANES), lora(LANES),
                  vec(width), vec(width), vec(width), vec(width), vec(width)],
        out_specs=[pl.BlockSpec((n_seq, chunk, width), lambda i, j: (i, j, 0)),
                   pl.BlockSpec((n_seq, n_pairs, LANES, LANES), lambda i, j: (i, 0, 0, 0)),
                   pl.BlockSpec((n_seq, 8, cols), lambda i, j: (i, 0, 0))],
        out_shape=[jax.ShapeDtypeStruct((b, tp, width), F32),
                   jax.ShapeDtypeStruct((b, n_pairs, LANES, LANES), F32),
                   jax.ShapeDtypeStruct((b, 8, cols), F32)],
        scratch_shapes=[pltpu.VMEM((n_seq, n_pairs, LANES, LANES), F32), pltpu.VMEM((n_seq, 8, cols), F32)],
        compiler_params=_params(2),
        name="rwkv7",
    )(rw, gates, ya, shift0, st0, p["mu"], p["w0"], p["a0"], p["w2"], p["a2"], p["g2"],
      p["k_k"], p["k_a"], p["r_k"], p["lnx_w"], p["lnx_b"])


def _cross_attention(x, g, wq, mk, mv, wo):
    q = jnp.dot(_rms(x, g).astype(BF16), wq, preferred_element_type=F32)
    hd = x.shape[1] // CA_HEADS
    outs = []
    for h in range(CA_HEADS):
        sl = slice(h * hd, (h + 1) * hd)
        s = lax.dot_general(q[:, sl].astype(BF16), mk[:, sl], (((1,), (1,)), ((), ())),
                            preferred_element_type=F32) * (hd ** -0.5)
        e = jnp.exp(s - jnp.max(s, axis=-1, keepdims=True))
        pr = e / jnp.sum(e, axis=-1, keepdims=True)
        outs.append(jnp.dot(pr.astype(BF16), mv[:, sl], preferred_element_type=F32))
    o = jnp.concatenate(outs, axis=-1)
    return x + jnp.dot(o.astype(BF16), wo, preferred_element_type=F32)


def _route(logits):
    lane = lax.broadcasted_iota(jnp.int32, logits.shape, 1)
    neg = jnp.float32(-jnp.inf)
    vals, idxs = [], []
    for _ in range(TOP_K):
        m = jnp.max(logits, axis=-1, keepdims=True)
        idx = jnp.min(jnp.where(logits == m, lane, LANES), axis=-1, keepdims=True)
        vals.append(m)
        idxs.append(idx)
        logits = jnp.where(lane == idx, neg, logits)
    es = [jnp.exp(vv - vals[0]) for vv in vals]
    den = functools.reduce(lambda p, q: p + q, es)
    gate = jnp.zeros(logits.shape, F32)
    sel = jnp.zeros(logits.shape, jnp.int32)
    for j in range(TOP_K):
        gate = jnp.where(lane == j, es[j] / den, gate)
        sel = jnp.where(lane == j, idxs[j], sel)
    return gate, sel


def _store_token_tiles(ref, rows):
    n, d = rows.shape
    sub = d // LANES
    for s in range(sub):
        ref[pl.ds(s, n, stride=sub), :] = rows[:, s * LANES:(s + 1) * LANES]


def _load_token_tiles(ref, sub):
    n = ref.shape[0] // sub
    return jnp.concatenate([ref[pl.ds(s, n, stride=sub), :] for s in range(sub)], axis=1)


def _mixer_out_body(mg_ref, x_ref, wout_ref, gca_ref, wq_ref, mk_ref, mv_ref, wo_ref, gffn_ref, wr_ref, br_ref,
                    x2_ref, h_ref, gate_ref, idx_ref):
    x1 = x_ref[...] + jnp.dot(mg_ref[...].astype(BF16), wout_ref[...], preferred_element_type=F32)
    x2 = _cross_attention(x1, gca_ref[...], wq_ref[...], mk_ref[0], mv_ref[0], wo_ref[...])
    x2_ref[...] = x2
    h = _rms(x2, gffn_ref[...])
    _store_token_tiles(h_ref, h)
    gate, sel = _route(_dot1(h, wr_ref[...]) + br_ref[...])
    gate_ref[...] = gate
    idx_ref[...] = sel


def _mixer_out(merged, x, mk, mv, w, t, tm):
    m, d = x.shape
    n_t = t // tm
    n_mem = mk.shape[1]
    sub = d // LANES
    resident = lambda a: pl.BlockSpec(a.shape, lambda i, j: (0,) * a.ndim, pipeline_mode=pl.Buffered(1))
    row_tile = lambda rows, n: pl.BlockSpec((rows, n), lambda i, j: (i * n_t + j, 0))
    mem = pl.BlockSpec((1, n_mem, d), lambda i, j: (i, 0, 0))
    ins = [merged, x, w["w_out"], w["norm_ca_g"], w["ca_wq"], mk, mv, w["ca_wo"], w["norm_ffn_g"],
           w["router_w"], w["router_b"]]
    return pl.pallas_call(
        _mixer_out_body,
        grid=(m // t, n_t),
        in_specs=[row_tile(tm, d), row_tile(tm, d), resident(ins[2]), resident(ins[3]), resident(ins[4]), mem, mem,
                  resident(ins[7]), resident(ins[8]), resident(ins[9]), resident(ins[10])],
        out_specs=[row_tile(tm, d), row_tile(tm * sub, LANES), row_tile(tm, LANES), row_tile(tm, LANES)],
        out_shape=[jax.ShapeDtypeStruct((m, d), F32), jax.ShapeDtypeStruct((m * sub, LANES), F32),
                   jax.ShapeDtypeStruct((m, LANES), F32), jax.ShapeDtypeStruct((m, LANES), jnp.int32)],
        compiler_params=_params(2),
        name="mixer_out",
    )(*ins)


def _wait_rows(make_copy, n_rows):
    for bit in reversed(range(MOE_BLOCK.bit_length())):
        size = 1 << bit

        @pl.when((n_rows & size) != 0)
        def _wait(size=size):
            make_copy(size).wait()


def _expert_body(blk_e_ref, n_valid_ref, n_used_ref, slot_ref, slot_next_ref, h_ref, wgu_ref, bgu_ref,
                 wdn_ref, bdn_ref, y_ref, xbuf, ybuf, gsem, ssem, *, n_blk):
    del blk_e_ref
    i = pl.program_id(0)
    n_used = n_used_ref[0]
    cur = lax.rem(i, 2)
    f = wdn_ref.shape[1]
    sub = wdn_ref.shape[2] // LANES
    m = h_ref.shape[0] // sub

    def token_of(slot):
        return (slot & (m - 1)) if m & (m - 1) == 0 else lax.rem(slot, m)

    def tiles(first, count=1):
        start = first * sub if isinstance(first, int) else pl.multiple_of(first * sub, sub)
        return pl.ds(start, count * sub)

    def for_rows(n_rows, row_fn):
        @pl.when(n_rows == MOE_BLOCK)
        def _full():
            def body(g, carry):
                for u in range(ROW_UNROLL):
                    row_fn(g * ROW_UNROLL + u, u % 2)
                return carry
            lax.fori_loop(0, MOE_BLOCK // ROW_UNROLL, body, 0)

        @pl.when(n_rows != MOE_BLOCK)
        def _partial():
            def body(r, carry):
                row_fn(r, 0)
                return carry
            lax.fori_loop(0, n_rows, body, 0)

    def gather_start(slots, n_rows, b):
        def row(r, priority):
            pltpu.make_async_copy(h_ref.at[tiles(token_of(slots[0, 0, r]))], xbuf.at[b, tiles(r)],
                                  gsem.at[b]).start(priority=priority)
        for_rows(n_rows, row)

    def gather_wait(n_rows, b):
        _wait_rows(lambda k: pltpu.make_async_copy(h_ref.at[tiles(0, k)], xbuf.at[b, tiles(0, k)], gsem.at[b]),
                   n_rows)

    def scatter_start(slots, n_rows, b):
        def row(r, priority):
            pltpu.make_async_copy(ybuf.at[b, tiles(r)], y_ref.at[tiles(slots[0, 0, r])],
                                  ssem.at[b]).start(priority=priority)
        for_rows(n_rows, row)

    def scatter_wait(n_rows, b):
        _wait_rows(lambda k: pltpu.make_async_copy(ybuf.at[b, tiles(0, k)], y_ref.at[tiles(0, k)], ssem.at[b]),
                   n_rows)

    @pl.when(i == 0)
    def _first():
        xbuf[...] = jnp.zeros(xbuf.shape, F32)
        gather_start(slot_ref, n_valid_ref[0], 0)

    @pl.when(i + 1 < n_used)
    def _prefetch():
        gather_start(slot_next_ref, n_valid_ref[i + 1], 1 - cur)

    @pl.when(i < n_used)
    def _run():
        gather_wait(n_valid_ref[i], cur)

        @pl.when(i >= 2)
        def _free_ybuf():
            scatter_wait(n_valid_ref[i - 2], cur)

        x = _load_token_tiles(xbuf.at[cur], sub)
        hh = jnp.dot(x.astype(BF16), wgu_ref[0], preferred_element_type=F32) + bgu_ref[0]
        hg = jnp.minimum(hh[:, :f], SWIGLU_LIMIT)
        hl = jnp.clip(hh[:, f:], -SWIGLU_LIMIT, SWIGLU_LIMIT)
        act = hg * _sigmoid(SWIGLU_ALPHA * hg) * (hl + 1.0)
        _store_token_tiles(ybuf.at[cur], jnp.dot(act.astype(BF16), wdn_ref[0], preferred_element_type=F32)
                           + bdn_ref[0])
        scatter_start(slot_ref, n_valid_ref[i], cur)

    @pl.when(i == n_blk - 1)
    def _drain():
        last = n_used - 1

        @pl.when(last >= 1)
        def _prev():
            scatter_wait(n_valid_ref[last - 1], lax.rem(last - 1, 2))

        @pl.when(last >= 0)
        def _last():
            scatter_wait(n_valid_ref[last], lax.rem(last, 2))


def _experts(h, slots, blk_e, n_valid, n_used, wgu, bgu, wdn, bdn):
    f, d = wdn.shape[1:]
    n_sub = d // LANES
    n_blk = slots.shape[0]
    grid_spec = pltpu.PrefetchScalarGridSpec(
        num_scalar_prefetch=3,
        grid=(n_blk,),
        in_specs=[pl.BlockSpec((1, 1, MOE_BLOCK), lambda i, be, nv, nu: (i, 0, 0), memory_space=pltpu.SMEM),
                  pl.BlockSpec((1, 1, MOE_BLOCK), lambda i, be, nv, nu: (jnp.minimum(i + 1, n_blk - 1), 0, 0),
                               memory_space=pltpu.SMEM),
                  pl.BlockSpec(memory_space=pl.ANY),
                  pl.BlockSpec((1, d, 2 * f), lambda i, be, nv, nu: (be[i], 0, 0)),
                  pl.BlockSpec((1, 1, 2 * f), lambda i, be, nv, nu: (be[i], 0, 0)),
                  pl.BlockSpec((1, f, d), lambda i, be, nv, nu: (be[i], 0, 0)),
                  pl.BlockSpec((1, 1, d), lambda i, be, nv, nu: (be[i], 0, 0))],
        out_specs=pl.BlockSpec(memory_space=pl.ANY),
        scratch_shapes=[pltpu.VMEM((2, MOE_BLOCK * n_sub, LANES), F32), pltpu.VMEM((2, MOE_BLOCK * n_sub, LANES), F32),
                        pltpu.SemaphoreType.DMA((2,)), pltpu.SemaphoreType.DMA((2,))],
    )
    return pl.pallas_call(
        functools.partial(_expert_body, n_blk=n_blk),
        grid_spec=grid_spec,
        out_shape=jax.ShapeDtypeStruct((h.shape[0] * TOP_K, LANES), F32),
        compiler_params=_params(),
        name="moe_experts",
    )(blk_e, n_valid, n_used, slots, slots, h, wgu, bgu, wdn, bdn)


def _combine_body(gate_ref, x_ref, fg_ref, *refs):
    y_refs, o_ref = refs[:TOP_K], refs[TOP_K]
    acc = x_ref[...]
    gate = gate_ref[...]
    sub = acc.shape[1] // LANES
    for k in range(TOP_K):
        acc = acc + gate[:, k:k + 1] * _load_token_tiles(y_refs[k], sub)
    o_ref[...] = _rms(acc, fg_ref[...])


def _combine(gate, x, fg, y, tm):
    m, d = x.shape
    n_sub = d // LANES
    n_tiles = m // tm
    y_spec = lambda k: pl.BlockSpec((tm * n_sub, LANES), lambda i: (k * n_tiles + i, 0))
    return pl.pallas_call(
        _combine_body,
        grid=(n_tiles,),
        in_specs=[pl.BlockSpec((tm, LANES), lambda i: (i, 0)),
                  pl.BlockSpec((tm, d), lambda i: (i, 0)),
                  pl.BlockSpec((1, d), lambda i: (0, 0))] + [y_spec(k) for k in range(TOP_K)],
        out_specs=pl.BlockSpec((tm, d), lambda i: (i, 0)),
        out_shape=jax.ShapeDtypeStruct((m, d), F32),
        compiler_params=_params(),
        name="moe_combine",
    )(gate, x, fg, *([y] * TOP_K))


def _moe_final(x2d, h, gate, sel, w, tm):
    m, d = x2d.shape
    flat_e = sel[:, :TOP_K].T.reshape(-1)
    n_assign = m * TOP_K
    order = jnp.argsort(flat_e, stable=True).astype(jnp.int32)
    experts = jnp.arange(N_EXPERTS, dtype=jnp.int32)
    counts = jnp.sum((flat_e[:, None] == experts[None, :]).astype(jnp.int32), axis=0)
    padded = (counts + MOE_BLOCK - 1) // MOE_BLOCK * MOE_BLOCK
    pad_end = jnp.cumsum(padded)
    pad_start = pad_end - padded
    start = jnp.cumsum(counts) - counts
    n_blk = -(-(n_assign + N_EXPERTS * (MOE_BLOCK - 1)) // MOE_BLOCK)
    blk_row0 = jnp.arange(n_blk, dtype=jnp.int32) * MOE_BLOCK
    blk_e = jnp.minimum(jnp.sum((blk_row0[:, None] >= pad_end[None, :]).astype(jnp.int32), axis=1), N_EXPERTS - 1)
    n_used = (pad_end[-1:] // MOE_BLOCK).astype(jnp.int32)
    off = blk_row0 - pad_start[blk_e]
    n_valid = jnp.clip(counts[blk_e] - off, 0, MOE_BLOCK).astype(jnp.int32)
    within = jnp.arange(MOE_BLOCK, dtype=jnp.int32)[None, :]
    pos = jnp.clip((start[blk_e] + off)[:, None] + within, 0, n_assign - 1)
    slots = jnp.where(within < n_valid[:, None], order[pos], 0).astype(jnp.int32).reshape(n_blk, 1, MOE_BLOCK)
    y = _experts(h, slots, blk_e, n_valid, n_used, w["moe_w_gu"], w["moe_b_gu"], w["moe_w_down"], w["moe_b_down"])
    return _combine(gate, x2d, w["final_norm_g"], y, tm)


def _layer(x, shift_prev, s0, mk, mv, w, emit_vn):
    b, t, d = x.shape
    m = b * t
    tm = min(256, m)
    x2d = x.reshape(m, d)
    rows = min(t, GMLP_CHUNK)
    ws = w["gmlp_ws"][:, :rows, :rows]
    bs_full = jnp.repeat(w["gmlp_bs"][:, :rows].T, d // GMLP_GROUPS, axis=1)
    ya, rw, gate_b, *v_n = _mixer_in(x2d, w["norm_mix_g"], w["w_in_a"], w["w_in_b"], w["gmlp_v_norm_g"], ws, bs_full,
                                     rows, tm, emit_vn)
    cols = rw.shape[1]

    n_heads = s0.shape[1]
    n_pairs = n_heads // 2
    tp = -(-t // RWKV_CHUNK) * RWKV_CHUNK
    pad3 = lambda z: jnp.pad(z.reshape(b, t, -1), ((0, 0), (0, tp - t), (0, 0)))
    st_t = jnp.swapaxes(s0, 2, 3).reshape(b, n_pairs, 2, RWKV_HEAD_DIM, RWKV_HEAD_DIM)
    zero = jnp.zeros_like(st_t[:, :, 0])
    st0 = jnp.concatenate([jnp.concatenate([st_t[:, :, 0], zero], axis=-1),
                           jnp.concatenate([zero, st_t[:, :, 1]], axis=-1)], axis=-2)
    shift0 = jnp.broadcast_to(shift_prev, (b, 8, cols))
    merged, st_out, last = _rwkv(pad3(rw), pad3(gate_b), pad3(ya), shift0, st0, w, t)
    merged = merged[:, :t].reshape(m, d)
    s_new = jnp.stack([st_out[:, :, :RWKV_HEAD_DIM, :RWKV_HEAD_DIM], st_out[:, :, RWKV_HEAD_DIM:, RWKV_HEAD_DIM:]],
                      axis=2).reshape(b, n_heads, RWKV_HEAD_DIM, RWKV_HEAD_DIM)
    s_new = jnp.swapaxes(s_new, 2, 3)
    shift_new = last[:, :1, :]

    x2, h, gate, sel = _mixer_out(merged, x2d, mk, mv, w, t, min(512, t))
    y = _moe_final(x2, h, gate, sel, w, min(256, m))
    return y.reshape(b, t, d), shift_new, s_new, (v_n[0].reshape(b, t, d) if emit_vn else None)


def kernel(x_prompt, x_sample, mem_prompt, state_shift, state_rwkv, cache_mem_k, cache_mem_v, norm_mix_g, w_in, gmlp_v_norm_g, gmlp_ws, gmlp_bs, rwkv_mu, rwkv_w0, rwkv_w2, rwkv_a0, rwkv_a2, rwkv_g2, rwkv_k_k, rwkv_k_a, rwkv_r_k, rwkv_lnx_w, rwkv_lnx_b, w_out, norm_ca_g, norm_mem_g, ca_wq, ca_wk, ca_wv, ca_wo, norm_ffn_g, router_w, router_b, moe_w_gu, moe_b_gu, moe_w_down, moe_b_down, final_norm_g):
    depth = w_in.shape[0]
    assert depth == 1, "the fused MoE + final-norm tail assumes a single layer"
    l = 0
    bp, _, d = x_prompt.shape
    width = rwkv_w0.shape[1]
    n_heads = width // RWKV_HEAD_DIM
    gw = gmlp_v_norm_g.shape[1]
    n_in = w_in.shape[2]
    row = lambda z: z.reshape(1, -1)
    zpad = jnp.zeros((LANES - rwkv_w2.shape[1], width), F32)
    w = dict(
        norm_mix_g=row(norm_mix_g[l]),
        w_in_a=jnp.concatenate([w_in[l][:, :2 * gw], w_in[l][:, n_in - 2 * d:n_in - d]], axis=1).astype(BF16),
        w_in_b=jnp.concatenate([w_in[l][:, 2 * gw:n_in - 2 * d], w_in[l][:, n_in - d:]], axis=1).astype(BF16),
        gmlp_v_norm_g=row(gmlp_v_norm_g[l]), gmlp_ws=gmlp_ws[l], gmlp_bs=gmlp_bs[l],
        mu=row(rwkv_mu[l]), w0=row(rwkv_w0[l]), a0=row(rwkv_a0[l]),
        w2=jnp.concatenate([rwkv_w2[l], zpad], axis=0).astype(BF16),
        a2=jnp.concatenate([zpad, rwkv_a2[l]], axis=0).astype(BF16),
        g2=rwkv_g2[l].astype(BF16),
        k_k=row(rwkv_k_k[l]), k_a=row(rwkv_k_a[l]), r_k=row(rwkv_r_k[l]),
        lnx_w=row(rwkv_lnx_w[l]), lnx_b=row(rwkv_lnx_b[l]),
        w_out=w_out[l].astype(BF16), norm_ca_g=row(norm_ca_g[l]),
        ca_wq=ca_wq[l].astype(BF16), ca_wo=ca_wo[l].astype(BF16),
        norm_ffn_g=row(norm_ffn_g[l]),
        router_w=jnp.pad(router_w[l], ((0, 0), (0, LANES - N_EXPERTS))).astype(BF16),
        router_b=jnp.pad(row(router_b[l]), ((0, 0), (0, LANES - N_EXPERTS)), constant_values=-jnp.inf),
        moe_w_gu=moe_w_gu[l].astype(BF16), moe_b_gu=moe_b_gu[l][:, None, :],
        moe_w_down=moe_w_down[l].astype(BF16), moe_b_down=moe_b_down[l][:, None, :],
        final_norm_g=row(final_norm_g),
    )

    n_mem = mem_prompt.shape[1]
    mem2d = mem_prompt.reshape(bp * n_mem, d)
    mk = _norm_matmul(mem2d, row(norm_mem_g[l]), ca_wk[l].astype(BF16), 256).reshape(bp, n_mem, d)
    mv = _norm_matmul(mem2d, row(norm_mem_g[l]), ca_wv[l].astype(BF16), 256).reshape(bp, n_mem, d)
    shift0 = jnp.zeros((bp, 1, rwkv_mu.shape[1]), F32)
    s_zero = jnp.zeros((bp, n_heads, RWKV_HEAD_DIM, RWKV_HEAD_DIM), F32)
    y_p, sh_p, st_p, _ = _layer(x_prompt, shift0, s_zero, mk.astype(BF16), mv.astype(BF16), w, emit_vn=False)

    bs = x_sample.shape[0]
    cmk = cache_mem_k[l].reshape(bs, n_mem, d).astype(BF16)
    cmv = cache_mem_v[l].reshape(bs, n_mem, d).astype(BF16)
    y_s, sh_s, st_s, v_s = _layer(x_sample, state_shift[l], state_rwkv[l], cmk, cmv, w, emit_vn=True)

    ca_shape = (1, bp, n_mem, CA_HEADS, d // CA_HEADS)
    return (y_p, y_s, sh_p[None], st_p[None], mk.reshape(ca_shape), mv.reshape(ca_shape),
            sh_s[None], st_s[None], v_s[None])
```
